```python
import math
import jax, jax.numpy as jnp
from jax import lax
import numpy as np

D_MODEL = 1024
BATCH = 8
SEQ = 4096
DEPTH = 2

CHUNK = 64
HEAD_DIM = 64
HEAD_UNIT = D_MODEL // (4 * HEAD_DIM)
SB_HEADS = HEAD_UNIT
DIFF_HEADS = HEAD_UNIT
CA_HEADS = HEAD_UNIT
SB_WIDTH = SB_HEADS * HEAD_DIM
DIFF_WIDTH = DIFF_HEADS * 2 * HEAD_DIM
CA_WIDTH = CA_HEADS * HEAD_DIM
MIX_WIDTH = SB_WIDTH + DIFF_WIDTH + CA_WIDTH
QKV_WIDTH = 3 * MIX_WIDTH
Q_BLOCK = 128
CA_LEFT_CHUNKS = 8
CA_BAND = (CA_LEFT_CHUNKS + 1) * CHUNK
CA_REL_CLIP = 128
T5_BUCKETS = 32
T5_MAX_DIST = 256
N_GROUPS = 4
EXPERTS_PER_GROUP = 8
N_EXPERTS = N_GROUPS * EXPERTS_PER_GROUP
TOP_K_INNER = 2
D_EXPERT = 512
MOE_BLOCK = 128
EPS = 1e-6

kernel_name = "hybrid_sb_diff_chunk_hmoe"


def rmsnorm(x, gain):
    x32 = x.astype(jnp.float32)
    y = x32 * lax.rsqrt(jnp.mean(x32 * x32, axis=-1, keepdims=True) + EPS)
    return (y * gain.astype(jnp.float32)).astype(x.dtype)


def split_heads(t, n_heads):
    b, s, _ = t.shape
    return t.reshape(b, s, n_heads, -1).transpose(0, 2, 1, 3)


def merge_heads(t):
    b, n, s, d = t.shape
    return t.transpose(0, 2, 1, 3).reshape(b, s, n * d)


def to_query_blocks(q):
    s, d = q.shape[-2], q.shape[-1]
    q = q.reshape(q.shape[:-2] + (s // Q_BLOCK, Q_BLOCK, d))
    return jnp.moveaxis(q, -3, 0)


def from_query_blocks(y):
    y = jnp.moveaxis(y, 0, -3)
    return y.reshape(y.shape[:-3] + (y.shape[-3] * y.shape[-2], y.shape[-1]))


def t5_bucket(rel):
    nb = T5_BUCKETS // 2
    max_exact = nb // 2
    ret = jnp.where(rel > 0, nb, 0)
    n = jnp.abs(rel)
    large = max_exact + (jnp.log(jnp.maximum(n, 1).astype(jnp.float32) / max_exact)
                         / math.log(T5_MAX_DIST / max_exact) * (nb - max_exact)).astype(jnp.int32)
    large = jnp.minimum(large, nb - 1)
    return ret + jnp.where(n < max_exact, n, large)


def stick_breaking_attention(q, k, v):
    s_len = q.shape[-2]
    scale = HEAD_DIM ** -0.5
    key_pos = jnp.arange(s_len)
    v32 = v.astype(jnp.float32)

    def block(args):
        qb, bi = args
        qpos = bi * Q_BLOCK + jnp.arange(Q_BLOCK)
        z = jnp.einsum('bhqd,bhkd->bhqk', qb, k, preferred_element_type=jnp.float32) * scale
        earlier = key_pos[None, :] < qpos[:, None]
        log_keep = jnp.where(earlier, jax.nn.log_sigmoid(-z), 0.0)
        between = lax.cumsum(log_keep, axis=3, reverse=True) - log_keep
        w = jnp.where(earlier, jnp.exp(jax.nn.log_sigmoid(z) + between), 0.0)
        return jnp.einsum('bhqk,bhkd->bhqd', w, v32)

    y = lax.map(block, (to_query_blocks(q), jnp.arange(s_len // Q_BLOCK)))
    return from_query_blocks(y)


def differential_attention(q, k, v, t5_bias, lam):
    s_len = q.shape[-2]
    scale = HEAD_DIM ** -0.5
    key_pos = jnp.arange(s_len)
    key_chunk = key_pos // CHUNK
    v32 = v.astype(jnp.float32)

    def block(args):
        qb, bi = args
        qpos = bi * Q_BLOCK + jnp.arange(Q_BLOCK)
        bias = t5_bias[t5_bucket(key_pos[None, :] - qpos[:, None])]
        bias = jnp.transpose(bias, (2, 0, 1)).astype(jnp.float32)
        allowed = key_chunk[None, :] <= (qpos // CHUNK)[:, None]
        logits = jnp.einsum('bhmqd,bhmkd->bhmqk', qb, k,
                            preferred_element_type=jnp.float32) * scale + bias[None, :, None]
        logits = jnp.where(allowed, logits, -jnp.inf)
        p = jax.nn.softmax(logits, axis=-1)
        w = p[:, :, 0] - lam * p[:, :, 1]
        return jnp.einsum('bhqk,bhkd->bhqd', w, v32)

    y = lax.map(block, (to_query_blocks(q), jnp.arange(s_len // Q_BLOCK)))
    return from_query_blocks(y)


def chunked_rel_attention(q, k, v, rel_bias):
    b, h, s_len, d = q.shape
    n_chunks = s_len // CHUNK
    scale = HEAD_DIM ** -0.5
    qc = q.reshape(b, h, n_chunks, CHUNK, d)
    pad = ((0, 0), (0, 0), (CA_LEFT_CHUNKS, 0), (0, 0), (0, 0))
    kp = jnp.pad(k.reshape(b, h, n_chunks, CHUNK, d), pad)
    vp = jnp.pad(v.reshape(b, h, n_chunks, CHUNK, d), pad)
    kb = jnp.concatenate([kp[:, :, j:j + n_chunks] for j in range(CA_LEFT_CHUNKS + 1)], axis=3)
    vb = jnp.concatenate([vp[:, :, j:j + n_chunks] for j in range(CA_LEFT_CHUNKS + 1)], axis=3)
    key_off = jnp.arange(CA_BAND) - CA_LEFT_CHUNKS * CHUNK
    rel = key_off[None, :] - jnp.arange(CHUNK)[:, None]
    bias = rel_bias[jnp.clip(rel, -CA_REL_CLIP, CA_REL_CLIP) + CA_REL_CLIP]
    bias = jnp.transpose(bias, (2, 0, 1)).astype(jnp.float32)
    band_chunk = jnp.arange(CA_BAND) // CHUNK
    valid = (jnp.arange(n_chunks)[:, None] - CA_LEFT_CHUNKS + band_chunk[None, :]) >= 0
    logits = jnp.einsum('bhnqd,bhnkd->bhnqk', qc, kb,
                        preferred_element_type=jnp.float32) * scale + bias[None, :, None]
    logits = jnp.where(valid[None, None, :, None, :], logits, -jnp.inf)
    p = jax.nn.softmax(logits, axis=-1)
    y = jnp.einsum('bhnqk,bhnkd->bhnqd', p, vb.astype(jnp.float32))
    return y.reshape(b, h, s_len, d)


def hybrid_mixer(a, w_in, w_out, t5_bias, diff_qk_norm, diff_lambda, diff_subln,
                 ca_qk_norm, ca_rel_bias, lam_init):
    b, s_len, _ = a.shape
    proj = a @ w_in
    sb, df, ca = jnp.split(proj, [3 * SB_WIDTH, 3 * SB_WIDTH + 3 * DIFF_WIDTH], axis=-1)

    qa, ka, va = [split_heads(t, SB_HEADS) for t in jnp.split(sb, 3, axis=-1)]
    ya = stick_breaking_attention(qa, ka, va)

    qd, kd, vd = jnp.split(df, 3, axis=-1)
    qd = rmsnorm(qd.reshape(b, s_len, DIFF_HEADS, 2, HEAD_DIM), diff_qk_norm[0]).transpose(0, 2, 3, 1, 4)
    kd = rmsnorm(kd.reshape(b, s_len, DIFF_HEADS, 2, HEAD_DIM), diff_qk_norm[1]).transpose(0, 2, 3, 1, 4)
    vd = split_heads(vd, DIFF_HEADS)
    lmb = diff_lambda.astype(jnp.float32)
    lam = jnp.exp(jnp.sum(lmb[0] * lmb[1])) - jnp.exp(jnp.sum(lmb[2] * lmb[3])) + lam_init
    yd = differential_attention(qd, kd, vd, t5_bias, lam)
    yd = rmsnorm(yd, diff_subln) * (1.0 - lam_init)

    qc, kc, vc = [split_heads(t, CA_HEADS) for t in jnp.split(ca, 3, axis=-1)]
    qc = rmsnorm(qc, ca_qk_norm[0])
    kc = rmsnorm(kc, ca_qk_norm[1])
    yc = chunked_rel_attention(qc, kc, vc, ca_rel_bias)

    merged = jnp.concatenate([merge_heads(ya), merge_heads(yd), merge_heads(yc)], axis=-1).astype(a.dtype)
    return merged @ w_out


def hierarchical_moe(f, wg_r, bg_r, we_r, be_r, w_gate, w_up, w_down):
    b, s_len, d = f.shape
    n_tok = b * s_len
    xt = f.reshape(n_tok, d)
    tok_idx = jnp.arange(n_tok)
    g_logits = (xt @ wg_r + bg_r).astype(jnp.float32)
    g_sel = jnp.argmax(g_logits, axis=-1)
    g_w = jax.nn.softmax(g_logits, axis=-1)[tok_idx, g_sel]
    e_logits = (xt @ we_r + be_r).astype(jnp.float32).reshape(n_tok, N_GROUPS, EXPERTS_PER_GROUP)
    e_prob = jax.nn.softmax(e_logits[tok_idx, g_sel], axis=-1)
    top_p, top_i = lax.top_k(e_prob, TOP_K_INNER)
    top_p = top_p / jnp.sum(top_p, axis=-1, keepdims=True)
    eid = (g_sel[:, None] * EXPERTS_PER_GROUP + top_i).reshape(-1)
    gate = (g_w[:, None] * top_p).reshape(-1)
    tok = jnp.repeat(tok_idx, TOP_K_INNER)
    n_asg = n_tok * TOP_K_INNER
    order = jnp.argsort(eid)
    eid_s, tok_s, gate_s = eid[order], tok[order], gate[order]
    counts = jnp.zeros((N_EXPERTS,), jnp.int32).at[eid].add(1)
    padded = (counts + MOE_BLOCK - 1) // MOE_BLOCK * MOE_BLOCK
    start = jnp.cumsum(counts) - counts
    pend = jnp.cumsum(padded)
    pstart = pend - padded
    dest = pstart[eid_s] + (jnp.arange(n_asg) - start[eid_s])
    n_rows = n_asg + N_EXPERTS * MOE_BLOCK
    n_blocks = n_rows // MOE_BLOCK
    row_tok = jnp.zeros((n_rows,), jnp.int32).at[dest].set(tok_s)
    row_gate = jnp.zeros((n_rows,), jnp.float32).at[dest].set(gate_s)
    blk_expert = jnp.minimum(jnp.searchsorted(pend, jnp.arange(n_blocks) * MOE_BLOCK, side='right'),
                             N_EXPERTS - 1)

    def expert_block(args):
        toks, gts, e = args
        xb = xt[toks]
        hid = jax.nn.silu(xb @ w_gate[e]) * (xb @ w_up[e])
        y = hid @ w_down[e]
        return y * gts[:, None].astype(y.dtype)

    y = lax.map(expert_block, (row_tok.reshape(n_blocks, MOE_BLOCK),
                               row_gate.reshape(n_blocks, MOE_BLOCK), blk_expert))
    y = y.reshape(n_rows, d)
    out = jnp.zeros((n_tok, d), y.dtype).at[row_tok].add(y)
    return out.reshape(b, s_len, d).astype(f.dtype)


def setup_inputs(seed: int = 0) -> dict:
    key = jax.random.key(seed)
    ks = jax.random.split(key, 20)
    f32 = jnp.float32

    def nrm(k, shape, scale):
        return jax.random.normal(k, shape, f32) * scale

    return {
        'x': nrm(ks[0], (BATCH, SEQ, D_MODEL), 1.0),
        't5_bias': nrm(ks[1], (T5_BUCKETS, DIFF_HEADS), 0.1),
        'attn_norm': 1.0 + nrm(ks[2], (DEPTH, D_MODEL), 0.02),
        'w_in': nrm(ks[3], (DEPTH, D_MODEL, QKV_WIDTH), D_MODEL ** -0.5),
        'diff_qk_norm': 1.0 + nrm(ks[4], (DEPTH, 2, HEAD_DIM), 0.02),
        'diff_lambda': nrm(ks[5], (DEPTH, 4, HEAD_DIM), 0.1),
        'diff_subln': 1.0 + nrm(ks[6], (DEPTH, 2 * HEAD_DIM), 0.02),
        'ca_qk_norm': 1.0 + nrm(ks[7], (DEPTH, 2, HEAD_DIM), 0.02),
        'ca_rel_bias': nrm(ks[8], (DEPTH, 2 * CA_REL_CLIP + 1, CA_HEADS), 0.1),
        'w_out': nrm(ks[9], (DEPTH, MIX_WIDTH, D_MODEL), MIX_WIDTH ** -0.5),
        'ffn_norm': 1.0 + nrm(ks[10], (DEPTH, D_MODEL), 0.02),
        'router_group_w': nrm(ks[11], (DEPTH, D_MODEL, N_GROUPS), D_MODEL ** -0.5),
        'router_group_b': nrm(ks[12], (DEPTH, N_GROUPS), 0.01),
        'router_expert_w': nrm(ks[13], (DEPTH, D_MODEL, N_EXPERTS), D_MODEL ** -0.5),
        'router_expert_b': nrm(ks[14], (DEPTH, N_EXPERTS), 0.01),
        'expert_w_gate': nrm(ks[15], (DEPTH, N_EXPERTS, D_MODEL, D_EXPERT), D_MODEL ** -0.5),
        'expert_w_up': nrm(ks[16], (DEPTH, N_EXPERTS, D_MODEL, D_EXPERT), D_MODEL ** -0.5),
        'expert_w_down': nrm(ks[17], (DEPTH, N_EXPERTS, D_EXPERT, D_MODEL), D_EXPERT ** -0.5),
    }


def reference(x, t5_bias, attn_norm, w_in, diff_qk_norm, diff_lambda, diff_subln, ca_qk_norm,
              ca_rel_bias, w_out, ffn_norm, router_group_w, router_group_b, router_expert_w,
              router_expert_b, expert_w_gate, expert_w_up, expert_w_down):
    h = x
    for l in range(DEPTH):
        lam_init = 0.8 - 0.6 * math.exp(-0.3 * l)
        a = rmsnorm(h, attn_norm[l])
        h = h + hybrid_mixer(a, w_in[l], w_out[l], t5_bias, diff_qk_norm[l], diff_lambda[l],
                             diff_subln[l], ca_qk_norm[l], ca_rel_bias[l], lam_init)
        f = rmsnorm(h, ffn_norm[l])
        h = h + hierarchical_moe(f, router_group_w[l], router_group_b[l], router_expert_w[l],
                                 router_expert_b[l], expert_w_gate[l], expert_w_up[l], expert_w_down[l])
    return h
```

```python
import functools
import math

import jax
import jax.numpy as jnp
from jax import lax
from jax.experimental import pallas as pl
from jax.experimental.pallas import tpu as pltpu

F32 = jnp.float32
BF16 = jnp.bfloat16

LANES = 128
HEAD_DIM = 64
CHUNK = 64
CA_LEFT_CHUNKS = 8
CA_REL_CLIP = 128
T5_BUCKETS = 32
T5_MAX_DIST = 256
N_GROUPS = 4
EXPERTS_PER_GROUP = 8
N_EXPERTS = N_GROUPS * EXPERTS_PER_GROUP
TOP_K_INNER = 2
EPS = 1e-6
NEG = -1e30
QK_SCALE = HEAD_DIM ** -0.5

ROW_TILE = 512
ATT_TILE = 256
MOE_TILE = 256
VMEM_LIMIT = 48 * 1024 * 1024


def _cparams(n_axes):
    return pltpu.CompilerParams(dimension_semantics=("arbitrary",) * n_axes,
                                vmem_limit_bytes=VMEM_LIMIT)


def _split_bf16(x):
    hi = x.astype(BF16)
    lo = (x - hi.astype(F32)).astype(BF16)
    return hi, lo


def _col_kind(col, d_model):
    sb_w, df_w = d_model // 4, d_model // 2
    sb_end = 3 * sb_w
    df_end = sb_end + 3 * df_w
    if col < sb_w:
        return "scale"
    if col < sb_end:
        return "plain"
    if col < sb_end + 2 * df_w:
        return "norm"
    if col < df_end:
        return "plain"
    if col < df_end + 2 * sb_w:
        return "norm"
    return "plain"


def _proj_kernel(*refs, with_moe, d_model, col_chunk):
    if with_moe:
        h_ref, y0_ref, y1_ref, g_ref, w_ref, cg_ref, gm_ref, hout_ref, o_ref = refs
        x = h_ref[...] + y0_ref[...] + y1_ref[...]
        hout_ref[...] = x
    else:
        h_ref, g_ref, w_ref, cg_ref, gm_ref, o_ref = refs
        x = h_ref[...]
    ms = jnp.mean(x * x, axis=-1, keepdims=True)
    xn = (x * lax.rsqrt(ms + EPS) * g_ref[...]).astype(BF16)
    n_out = w_ref.shape[1]
    gm = gm_ref[...]
    for c0 in range(0, n_out, col_chunk):
        acc = jnp.dot(xn, w_ref[:, c0:c0 + col_chunk], preferred_element_type=F32)
        for s0 in range(0, col_chunk, LANES):
            col = c0 + s0
            blk = acc[:, s0:s0 + LANES]
            kind = _col_kind(col, d_model)
            if kind == "norm":
                hi, lo = _split_bf16(blk * blk)
                msq = (jnp.dot(hi, gm, preferred_element_type=F32)
                       + jnp.dot(lo, gm, preferred_element_type=F32))
                blk = blk * lax.rsqrt(msq + EPS) * cg_ref[:, col:col + LANES]
            elif kind == "scale":
                blk = blk * QK_SCALE
            o_ref[:, col:col + LANES] = blk.astype(BF16)


def _proj_call(h, y_pair, gain, w_bf, col_gain, group_mean):
    n_tok, d_model = h.shape
    n_out = w_bf.shape[1]
    tm = ROW_TILE
    with_moe = y_pair is not None
    row = pl.BlockSpec((tm, d_model), lambda i: (i, 0))
    in_specs = [row]
    args = [h]
    if with_moe:
        in_specs += [pl.BlockSpec((tm, d_model), lambda i: (i, 0)),
                     pl.BlockSpec((tm, d_model), lambda i: (i, 1))]
        args += [y_pair, y_pair]
    in_specs += [pl.BlockSpec((1, d_model), lambda i: (0, 0)),
                 pl.BlockSpec((d_model, n_out), lambda i: (0, 0)),
                 pl.BlockSpec((1, n_out), lambda i: (0, 0)),
                 pl.BlockSpec((LANES, LANES), lambda i: (0, 0))]
    args += [gain.reshape(1, d_model), w_bf, col_gain, group_mean]
    proj_spec = pl.BlockSpec((tm, n_out), lambda i: (i, 0))
    proj_shape = jax.ShapeDtypeStruct((n_tok, n_out), BF16)
    if with_moe:
        out_specs = [row, proj_spec]
        out_shape = [jax.ShapeDtypeStruct((n_tok, d_model), F32), proj_shape]
    else:
        out_specs = proj_spec
        out_shape = proj_shape
    out = pl.pallas_call(
        functools.partial(_proj_kernel, with_moe=with_moe, d_model=d_model, col_chunk=512),
        grid=(n_tok // tm,), in_specs=in_specs, out_specs=out_specs, out_shape=out_shape,
        compiler_params=_cparams(1), name="norm_in_proj")(*args)
    if with_moe:
        return out[0], out[1]
    return h, out


def _lane_halves(q):
    lane = lax.broadcasted_iota(jnp.int32, q.shape, 1)
    zero = jnp.zeros_like(q)
    return jnp.where(lane < HEAD_DIM, q, zero), jnp.where(lane >= HEAD_DIM, q, zero)


def _scores(qh, k):
    return lax.dot_general(qh, k, (((1,), (1,)), ((), ())), preferred_element_type=F32)


def _lane_tile(x, width):
    reps = width // LANES
    return x if reps == 1 else jnp.concatenate([x] * reps, axis=1)


def _kv_tile(k_ref, v_ref, j, tk):
    start = pl.multiple_of(j * tk, tk)
    return k_ref[0, pl.ds(start, tk), :], v_ref[0, pl.ds(start, tk), :]


def _softmax_update(s, v, idx, m_ref, l_ref, acc_ref):
    m_prev = m_ref[idx]
    m_next = jnp.maximum(m_prev, jnp.max(s, axis=1, keepdims=True))
    alpha = jnp.exp(m_prev - m_next)
    p = jnp.exp(s - _lane_tile(m_next, s.shape[1]))
    l_ref[idx] = alpha * l_ref[idx] + jnp.sum(p, axis=1, keepdims=True)
    acc_ref[idx] = alpha * acc_ref[idx] + jnp.dot(p.astype(BF16), v, preferred_element_type=F32)
    m_ref[idx] = m_next


def _init_softmax(m_ref, l_ref, acc_ref):
    m_ref[...] = jnp.full(m_ref.shape, NEG, F32)
    l_ref[...] = jnp.zeros(l_ref.shape, F32)
    acc_ref[...] = jnp.zeros(acc_ref.shape, F32)


def _sb_kernel(q_ref, k_ref, v_ref, u_ref, o_ref, acc_ref, carry_ref, *, tq):
    i = pl.program_id(2)
    q_halves = _lane_halves(q_ref[0])
    u = u_ref[...]
    acc_ref[...] = jnp.zeros(acc_ref.shape, F32)
    carry_ref[...] = jnp.zeros(carry_ref.shape, F32)

    def tile(j, diagonal):
        k, v = _kv_tile(k_ref, v_ref, j, tq)
        if diagonal:
            row = lax.broadcasted_iota(jnp.int32, (tq, tq), 0)
            col = lax.broadcasted_iota(jnp.int32, (tq, tq), 1)
            earlier = col < row
        for idx, qh in enumerate(q_halves):
            z = _scores(qh, k)
            softplus = jnp.maximum(z, 0.0) + jnp.log1p(jnp.exp(-jnp.abs(z)))
            log_keep = -softplus
            if diagonal:
                log_keep = jnp.where(earlier, log_keep, 0.0)
            hi, lo = _split_bf16(log_keep)
            between = (jnp.dot(hi, u, preferred_element_type=F32)
                       + jnp.dot(lo, u, preferred_element_type=F32))
            carry = carry_ref[idx]
            w = jnp.exp(z - softplus + between + _lane_tile(carry, tq))
            if diagonal:
                w = jnp.where(earlier, w, 0.0)
            acc_ref[idx] += jnp.dot(w.astype(BF16), v, preferred_element_type=F32)
            carry_ref[idx] = carry + jnp.sum(log_keep, axis=1, keepdims=True)

    tile(i, True)

    def body(jj, c):
        tile(i - 1 - jj, False)
        return c

    lax.fori_loop(0, i, body, 0)
    lane = lax.broadcasted_iota(jnp.int32, (tq, LANES), 1)
    o_ref[0] = jnp.where(lane < HEAD_DIM, acc_ref[0], acc_ref[1]).astype(o_ref.dtype)


def _diff_kernel(far_ref, q_ref, k_ref, v_ref, near_ref, lam_ref, subln_ref, o_ref,
                 m_ref, l_ref, acc_ref, *, tq, lam_init):
    h = pl.program_id(1)
    i = pl.program_id(2)
    q_maps = _lane_halves(q_ref[0])
    _init_softmax(m_ref, l_ref, acc_ref)

    def tile(j, bias):
        k, v = _kv_tile(k_ref, v_ref, j, tq)
        for idx, qm in enumerate(q_maps):
            _softmax_update(_scores(qm, k) + bias, v, idx, m_ref, l_ref, acc_ref)

    tile(i, near_ref[0, 0])

    @pl.when(i >= 1)
    def _():
        tile(i - 1, near_ref[0, 1])

    far = far_ref[h]

    def body(j, c):
        tile(j, far)
        return c

    lax.fori_loop(0, jnp.maximum(i - 1, 0), body, 0)

    lmb = lam_ref[...]
    lam = (jnp.exp(jnp.sum(lmb[0:1] * lmb[1:2], axis=-1, keepdims=True))
           - jnp.exp(jnp.sum(lmb[2:3] * lmb[3:4], axis=-1, keepdims=True)) + lam_init)
    y = acc_ref[0] / l_ref[0] - lam * (acc_ref[1] / l_ref[1])
    ms = jnp.mean(y * y, axis=-1, keepdims=True)
    y = y * lax.rsqrt(ms + EPS) * subln_ref[...] * (1.0 - lam_init)
    o_ref[0] = y.astype(o_ref.dtype)


def _ca_kernel(q_ref, k_ref, v_ref, bias_ref, o_ref, m_ref, l_ref, acc_ref, *, tq, n_near):
    i = pl.program_id(2)
    q_halves = _lane_halves(q_ref[0])
    _init_softmax(m_ref, l_ref, acc_ref)

    def tile(d):
        k, v = _kv_tile(k_ref, v_ref, i - d, tq)
        for idx, qh in enumerate(q_halves):
            _softmax_update(_scores(qh, k) + bias_ref[idx, d], v, idx, m_ref, l_ref, acc_ref)

    tile(0)
    for d in range(1, n_near):
        pl.when(i >= d)(functools.partial(tile, d))

    lane = lax.broadcasted_iota(jnp.int32, (tq, LANES), 1)
    y = jnp.where(lane < HEAD_DIM, acc_ref[0] / l_ref[0], acc_ref[1] / l_ref[1])
    o_ref[0] = y.astype(o_ref.dtype)


def _t5_bucket(rel):
    nb = T5_BUCKETS // 2
    max_exact = nb // 2
    ret = jnp.where(rel > 0, nb, 0)
    n = jnp.abs(rel)
    large = max_exact + (jnp.log(jnp.maximum(n, 1).astype(F32) / max_exact)
                         / math.log(T5_MAX_DIST / max_exact) * (nb - max_exact)).astype(jnp.int32)
    large = jnp.minimum(large, nb - 1)
    return ret + jnp.where(n < max_exact, n, large)


def _diff_bias_tables(t5_bias, tq):
    assert tq >= T5_MAX_DIST and tq % CHUNK == 0
    r = jnp.arange(tq)[:, None]
    c = jnp.arange(tq)[None, :]
    tiles = []
    for d in (0, 1):
        b = jnp.transpose(t5_bias[_t5_bucket(c - r - d * tq)], (2, 0, 1)).astype(F32)
        if d == 0:
            b = jnp.where((c // CHUNK) <= (r // CHUNK), b, NEG)
        tiles.append(b)
    far = t5_bias[_t5_bucket(jnp.int32(-2 * tq))].astype(F32)
    return jnp.stack(tiles, axis=1), far


def _ca_bias_tables(rel_bias, tq):
    assert (CA_LEFT_CHUNKS * CHUNK) % tq == 0 and tq % CHUNK == 0
    n_near = CA_LEFT_CHUNKS * CHUNK // tq + 1
    r = jnp.arange(tq)[:, None]
    c = jnp.arange(tq)[None, :]
    tiles = []
    for d in range(n_near):
        rel = c - r - d * tq
        b = rel_bias[jnp.clip(rel, -CA_REL_CLIP, CA_REL_CLIP) + CA_REL_CLIP]
        b = jnp.transpose(b, (2, 0, 1)).astype(F32)
        gap = d * (tq // CHUNK) + r // CHUNK - c // CHUNK
        tiles.append(jnp.where((gap >= 0) & (gap <= CA_LEFT_CHUNKS), b, NEG))
    return jnp.stack(tiles, axis=1), n_near


def _attention_calls(proj, batch, t5_bias, diff_lambda, diff_subln, ca_rel_bias, lam_init):
    n_tok, n_proj = proj.shape
    seq = n_tok // batch
    d_model = n_proj // 3
    sb_w = d_model // 4
    proj3 = proj.reshape(batch, seq, n_proj)
    tq = ATT_TILE
    nq = seq // tq
    sb_blocks = sb_w // LANES
    df_heads = (d_model // 2) // LANES
    sb_q0 = 0
    df_q0 = 3 * sb_blocks
    ca_q0 = df_q0 + 3 * df_heads

    def qkv_specs(q0, stride):
        return [pl.BlockSpec((1, tq, LANES), lambda b, h, i: (b, i, q0 + h)),
                pl.BlockSpec((1, seq, LANES), lambda b, h, i: (b, 0, q0 + stride + h)),
                pl.BlockSpec((1, seq, LANES), lambda b, h, i: (b, 0, q0 + 2 * stride + h))]

    out_spec = pl.BlockSpec((1, tq, LANES), lambda b, h, i: (b, i, h))
    state = lambda n: pltpu.VMEM((2, tq, LANES), F32)

    u = (jnp.arange(tq)[:, None] > jnp.arange(tq)[None, :]).astype(BF16)
    ya = pl.pallas_call(
        functools.partial(_sb_kernel, tq=tq),
        grid=(batch, sb_blocks, nq),
        in_specs=qkv_specs(sb_q0, sb_blocks) + [pl.BlockSpec((tq, tq), lambda b, h, i: (0, 0))],
        out_specs=out_spec,
        out_shape=jax.ShapeDtypeStruct((batch, seq, sb_w), BF16),
        scratch_shapes=[state(0), state(1)],
        compiler_params=_cparams(3), name="stick_breaking_attn")(proj3, proj3, proj3, u)

    near, far = _diff_bias_tables(t5_bias, tq)
    yd = pl.pallas_call(
        functools.partial(_diff_kernel, tq=tq, lam_init=lam_init),
        grid=(batch, df_heads, nq),
        in_specs=[pl.BlockSpec(memory_space=pltpu.SMEM)] + qkv_specs(df_q0, df_heads) + [
            pl.BlockSpec((1, 2, tq, tq), lambda b, h, i: (h, 0, 0, 0)),
            pl.BlockSpec((4, HEAD_DIM), lambda b, h, i: (0, 0)),
            pl.BlockSpec((1, LANES), lambda b, h, i: (0, 0))],
        out_specs=out_spec,
        out_shape=jax.ShapeDtypeStruct((batch, seq, d_model // 2), BF16),
        scratch_shapes=[state(0), state(1), state(2)],
        compiler_params=_cparams(3), name="differential_attn")(
            far, proj3, proj3, proj3, near, diff_lambda.astype(F32),
            diff_subln.astype(F32).reshape(1, LANES))

    ca_bias, n_near = _ca_bias_tables(ca_rel_bias, tq)
    yc = pl.pallas_call(
        functools.partial(_ca_kernel, tq=tq, n_near=n_near),
        grid=(batch, sb_blocks, nq),
        in_specs=qkv_specs(ca_q0, sb_blocks) + [
            pl.BlockSpec((2, n_near, tq, tq), lambda b, h, i: (h, 0, 0, 0))],
        out_specs=out_spec,
        out_shape=jax.ShapeDtypeStruct((batch, seq, sb_w), BF16),
        scratch_shapes=[state(0), state(1), state(2)],
        compiler_params=_cparams(3), name="chunked_rel_attn")(proj3, proj3, proj3, ca_bias)

    return (ya.reshape(n_tok, sb_w), yd.reshape(n_tok, d_model // 2), yc.reshape(n_tok, sb_w))


def _out_proj_kernel(h_ref, ya_ref, yd_ref, yc_ref, wo_ref, g_ref, rhi_ref, rlo_ref, rb_ref,
                     h2_ref, f_ref, logit_ref):
    wa = ya_ref.shape[1]
    wd = yd_ref.shape[1]
    h2 = (h_ref[...]
          + jnp.dot(ya_ref[...], wo_ref[0:wa, :], preferred_element_type=F32)
          + jnp.dot(yd_ref[...], wo_ref[wa:wa + wd, :], preferred_element_type=F32)
          + jnp.dot(yc_ref[...], wo_ref[wa + wd:, :], preferred_element_type=F32))
    h2_ref[...] = h2
    ms = jnp.mean(h2 * h2, axis=-1, keepdims=True)
    f = h2 * lax.rsqrt(ms + EPS) * g_ref[...]
    f_hi, f_lo = _split_bf16(f)
    f_ref[...] = f_hi
    logit_ref[...] = (jnp.dot(f_hi, rhi_ref[...], preferred_element_type=F32)
                      + jnp.dot(f_lo, rhi_ref[...], preferred_element_type=F32)
                      + jnp.dot(f_hi, rlo_ref[...], preferred_element_type=F32)
                      + rb_ref[...])


def _out_proj_call(h, ya, yd, yc, wo_bf, gain, r_hi, r_lo, r_bias):
    n_tok, d_model = h.shape
    tm = ROW_TILE
    row = lambda w: pl.BlockSpec((tm, w), lambda i: (i, 0))
    full = lambda a: pl.BlockSpec(a.shape, lambda i: (0, 0))
    gain = gain.reshape(1, d_model)
    return pl.pallas_call(
        _out_proj_kernel,
        grid=(n_tok // tm,),
        in_specs=[row(d_model), row(ya.shape[1]), row(yd.shape[1]), row(yc.shape[1]),
                  full(wo_bf), full(gain), full(r_hi), full(r_lo), full(r_bias)],
        out_specs=[row(d_model), row(d_model), row(LANES)],
        out_shape=[jax.ShapeDtypeStruct((n_tok, d_model), F32),
                   jax.ShapeDtypeStruct((n_tok, d_model), BF16),
                   jax.ShapeDtypeStruct((n_tok, LANES), F32)],
        compiler_params=_cparams(1), name="out_proj_router")(
            h, ya, yd, yc, wo_bf, gain, r_hi, r_lo, r_bias)


def _expert_kernel(blk_expert_ref, x_ref, gate_ref, wg_ref, wu_ref, wd_ref, y_ref):
    del blk_expert_ref
    x = x_ref[...]
    g = jnp.dot(x, wg_ref[0], preferred_element_type=F32)
    u = jnp.dot(x, wu_ref[0], preferred_element_type=F32)
    hid = (g / (1.0 + jnp.exp(-g)) * u).astype(BF16)
    y = jnp.dot(hid, wd_ref[0], preferred_element_type=F32)
    y_ref[...] = y * _lane_tile(gate_ref[...], y.shape[1])


def _expert_call(xs, row_gate, blk_expert, wg_bf, wu_bf, wd_bf):
    n_rows, d_model = xs.shape
    d_exp = wg_bf.shape[2]
    tb = MOE_TILE
    grid_spec = pltpu.PrefetchScalarGridSpec(
        num_scalar_prefetch=1,
        grid=(n_rows // tb,),
        in_specs=[pl.BlockSpec((tb, d_model), lambda i, be: (i, 0)),
                  pl.BlockSpec((tb, LANES), lambda i, be: (i, 0)),
                  pl.BlockSpec((1, d_model, d_exp), lambda i, be: (be[i], 0, 0)),
                  pl.BlockSpec((1, d_model, d_exp), lambda i, be: (be[i], 0, 0)),
                  pl.BlockSpec((1, d_exp, d_model), lambda i, be: (be[i], 0, 0))],
        out_specs=pl.BlockSpec((tb, d_model), lambda i, be: (i, 0)))
    return pl.pallas_call(
        _expert_kernel, grid_spec=grid_spec,
        out_shape=jax.ShapeDtypeStruct((n_rows, d_model), F32),
        compiler_params=_cparams(1), name="expert_mlp")(
            blk_expert, xs, row_gate, wg_bf, wu_bf, wd_bf)


def _route(logits, tb):
    n_tok = logits.shape[0]
    tok_idx = jnp.arange(n_tok)
    g_logits = logits[:, :N_GROUPS]
    g_sel = jnp.argmax(g_logits, axis=-1)
    g_w = jax.nn.softmax(g_logits, axis=-1)[tok_idx, g_sel]
    e_logits = logits[:, N_GROUPS:N_GROUPS + N_EXPERTS].reshape(n_tok, N_GROUPS, EXPERTS_PER_GROUP)
    e_prob = jax.nn.softmax(e_logits[tok_idx, g_sel], axis=-1)
    top_p, top_i = lax.top_k(e_prob, TOP_K_INNER)
    top_p = top_p / jnp.sum(top_p, axis=-1, keepdims=True)
    eid = (g_sel[:, None] * EXPERTS_PER_GROUP + top_i).reshape(-1).astype(jnp.int32)
    gate = (g_w[:, None] * top_p).reshape(-1)
    n_asg = n_tok * TOP_K_INNER
    order = jnp.argsort(eid).astype(jnp.int32)
    eid_s = eid[order]
    counts = jnp.zeros((N_EXPERTS,), jnp.int32).at[eid].add(1)
    padded = (counts + tb - 1) // tb * tb
    start = jnp.cumsum(counts) - counts
    pend = jnp.cumsum(padded)
    pstart = pend - padded
    dest_s = pstart[eid_s] + (jnp.arange(n_asg, dtype=jnp.int32) - start[eid_s])
    n_rows = n_asg + N_EXPERTS * tb
    n_blocks = n_rows // tb
    row_asg = jnp.full((n_rows,), n_asg, jnp.int32).at[dest_s].set(order)
    valid = row_asg < n_asg
    safe_asg = jnp.where(valid, row_asg, 0)
    row_tok = safe_asg // TOP_K_INNER
    row_gate = jnp.where(valid, gate[safe_asg], 0.0)
    asg_row = jnp.zeros((n_asg,), jnp.int32).at[order].set(dest_s)
    blk_expert = jnp.minimum(
        jnp.searchsorted(pend, jnp.arange(n_blocks, dtype=jnp.int32) * tb, side="right"),
        N_EXPERTS - 1).astype(jnp.int32)
    return row_tok, row_gate, asg_row, blk_expert


def _moe(f_bf, logits, wg_bf, wu_bf, wd_bf):
    n_tok, d_model = f_bf.shape
    row_tok, row_gate, asg_row, blk_expert = _route(logits, MOE_TILE)
    xs = f_bf[row_tok]
    gate_b = jnp.broadcast_to(row_gate[:, None], (row_gate.shape[0], LANES))
    y_rows = _expert_call(xs, gate_b, blk_expert, wg_bf, wu_bf, wd_bf)
    return y_rows[asg_row].reshape(n_tok, TOP_K_INNER * d_model)


def _combine_kernel(h_ref, y0_ref, y1_ref, o_ref):
    o_ref[...] = h_ref[...] + y0_ref[...] + y1_ref[...]


def _combine_call(h2, y_pair):
    n_tok, d_model = h2.shape
    tm = ROW_TILE
    return pl.pallas_call(
        _combine_kernel, grid=(n_tok // tm,),
        in_specs=[pl.BlockSpec((tm, d_model), lambda i: (i, 0)),
                  pl.BlockSpec((tm, d_model), lambda i: (i, 0)),
                  pl.BlockSpec((tm, d_model), lambda i: (i, 1))],
        out_specs=pl.BlockSpec((tm, d_model), lambda i: (i, 0)),
        out_shape=jax.ShapeDtypeStruct((n_tok, d_model), F32),
        compiler_params=_cparams(1), name="moe_combine")(h2, y_pair, y_pair)


def _proj_col_gain(diff_qk_norm, ca_qk_norm, d_model):
    sb_w, df_w = d_model // 4, d_model // 2
    ones = lambda n: jnp.ones((n,), F32)
    rep = lambda g, n: jnp.tile(g.astype(F32), n // HEAD_DIM)
    return jnp.concatenate([
        ones(3 * sb_w),
        rep(diff_qk_norm[0], df_w) * QK_SCALE, rep(diff_qk_norm[1], df_w), ones(df_w),
        rep(ca_qk_norm[0], sb_w) * QK_SCALE, rep(ca_qk_norm[1], sb_w), ones(sb_w)]).reshape(1, -1)


def kernel(x, t5_bias, attn_norm, w_in, diff_qk_norm, diff_lambda, diff_subln, ca_qk_norm,
           ca_rel_bias, w_out, ffn_norm, router_group_w, router_group_b, router_expert_w,
           router_expert_b, expert_w_gate, expert_w_up, expert_w_down):
    batch, seq, d_model = x.shape
    depth = w_in.shape[0]
    n_tok = batch * seq
    lane_group = jnp.arange(LANES) // HEAD_DIM
    group_mean = ((lane_group[:, None] == lane_group[None, :]).astype(F32) / HEAD_DIM).astype(BF16)

    h = x.reshape(n_tok, d_model)
    y_pair = None
    for l in range(depth):
        lam_init = 0.8 - 0.6 * math.exp(-0.3 * l)
        col_gain = _proj_col_gain(diff_qk_norm[l], ca_qk_norm[l], d_model)
        h, proj = _proj_call(h, y_pair, attn_norm[l], w_in[l].astype(BF16), col_gain, group_mean)
        ya, yd, yc = _attention_calls(proj, batch, t5_bias, diff_lambda[l], diff_subln[l],
                                      ca_rel_bias[l], lam_init)
        r_w = jnp.concatenate([router_group_w[l], router_expert_w[l]], axis=1).astype(F32)
        r_w = jnp.pad(r_w, ((0, 0), (0, LANES - r_w.shape[1])))
        r_hi, r_lo = _split_bf16(r_w)
        r_b = jnp.concatenate([router_group_b[l], router_expert_b[l]]).astype(F32)
        r_b = jnp.pad(r_b, (0, LANES - r_b.shape[0])).reshape(1, LANES)
        h, f_bf, logits = _out_proj_call(h, ya, yd, yc, w_out[l].astype(BF16), ffn_norm[l],
                                         r_hi, r_lo, r_b)
        y_pair = _moe(f_bf, logits, expert_w_gate[l].astype(BF16), expert_w_up[l].astype(BF16),
                      expert_w_down[l].astype(BF16))
    return _combine_call(h, y_pair).reshape(batch, seq, d_model)
```

```python
import functools
import math

import jax
import jax.numpy as jnp
from jax import lax
from jax.experimental import pallas as pl
from jax.experimental.pallas import tpu as pltpu

F32 = jnp.float32
BF16 = jnp.bfloat16

LANES = 128
HEAD_DIM = 64
CHUNK = 64
CA_LEFT_CHUNKS = 8
CA_REL_CLIP = 128
T5_BUCKETS = 32
T5_MAX_DIST = 256
N_GROUPS = 4
EXPERTS_PER_GROUP = 8
N_EXPERTS = N_GROUPS * EXPERTS_PER_GROUP
TOP_K_INNER = 2
EPS = 1e-6
NEG = -1e30
LOG2E = math.log2(math.e)
Q_FOLD = HEAD_DIM ** -0.5 * LOG2E

ROW_TILE = 512
ATT_TILE = 256
MOE_TILE = 256
VMEM_LIMIT = 48 * 1024 * 1024


def _cparams(n_axes):
    return pltpu.CompilerParams(dimension_semantics=("arbitrary",) * n_axes,
                                vmem_limit_bytes=VMEM_LIMIT)


def _split_bf16(x):
    hi = x.astype(BF16)
    lo = (x - hi.astype(F32)).astype(BF16)
    return hi, lo


def _col_kind(col, d_model):
    sb_w, df_w = d_model // 4, d_model // 2
    sb_end = 3 * sb_w
    df_end = sb_end + 3 * df_w
    if col < sb_w:
        return "scale"
    if col < sb_end:
        return "plain"
    if col < sb_end + 2 * df_w:
        return "norm"
    if col < df_end:
        return "plain"
    if col < df_end + 2 * sb_w:
        return "norm"
    return "plain"


def _proj_kernel(*refs, with_moe, d_model, col_chunk):
    if with_moe:
        h_ref, y0_ref, y1_ref, g_ref, w_ref, cg_ref, gm_ref, hout_ref, o_ref = refs
        x = h_ref[...] + y0_ref[...] + y1_ref[...]
        hout_ref[...] = x
    else:
        h_ref, g_ref, w_ref, cg_ref, gm_ref, o_ref = refs
        x = h_ref[...]
    ms = jnp.mean(x * x, axis=-1, keepdims=True)
    xn = (x * lax.rsqrt(ms + EPS) * g_ref[...]).astype(BF16)
    n_out = w_ref.shape[1]
    gm = gm_ref[...]
    for c0 in range(0, n_out, col_chunk):
        acc = jnp.dot(xn, w_ref[:, c0:c0 + col_chunk], preferred_element_type=F32)
        for s0 in range(0, col_chunk, LANES):
            col = c0 + s0
            blk = acc[:, s0:s0 + LANES]
            kind = _col_kind(col, d_model)
            if kind == "norm":
                hi, lo = _split_bf16(blk * blk)
                msq = (jnp.dot(hi, gm, preferred_element_type=F32)
                       + jnp.dot(lo, gm, preferred_element_type=F32))
                blk = blk * lax.rsqrt(msq + EPS) * cg_ref[:, col:col + LANES]
            elif kind == "scale":
                blk = blk * Q_FOLD
            o_ref[:, col:col + LANES] = blk.astype(BF16)


def _proj_call(h, y_pair, gain, w_bf, col_gain, group_mean):
    n_tok, d_model = h.shape
    n_out = w_bf.shape[1]
    tm = ROW_TILE
    with_moe = y_pair is not None
    row = pl.BlockSpec((tm, d_model), lambda i: (i, 0))
    in_specs = [row]
    args = [h]
    if with_moe:
        in_specs += [pl.BlockSpec((tm, d_model), lambda i: (i, 0)),
                     pl.BlockSpec((tm, d_model), lambda i: (i, 1))]
        args += [y_pair, y_pair]
    in_specs += [pl.BlockSpec((1, d_model), lambda i: (0, 0)),
                 pl.BlockSpec((d_model, n_out), lambda i: (0, 0)),
                 pl.BlockSpec((1, n_out), lambda i: (0, 0)),
                 pl.BlockSpec((LANES, LANES), lambda i: (0, 0))]
    args += [gain.reshape(1, d_model), w_bf, col_gain, group_mean]
    proj_spec = pl.BlockSpec((tm, n_out), lambda i: (i, 0))
    proj_shape = jax.ShapeDtypeStruct((n_tok, n_out), BF16)
    if with_moe:
        out_specs = [row, proj_spec]
        out_shape = [jax.ShapeDtypeStruct((n_tok, d_model), F32), proj_shape]
    else:
        out_specs = proj_spec
        out_shape = proj_shape
    out = pl.pallas_call(
        functools.partial(_proj_kernel, with_moe=with_moe, d_model=d_model, col_chunk=512),
        grid=(n_tok // tm,), in_specs=in_specs, out_specs=out_specs, out_shape=out_shape,
        compiler_params=_cparams(1), name="norm_in_proj")(*args)
    if with_moe:
        return out[0], out[1]
    return h, out


def _stack_halves(q):
    lane = lax.broadcasted_iota(jnp.int32, q.shape, 1)
    zero = jnp.zeros_like(q)
    return jnp.concatenate([jnp.where(lane < HEAD_DIM, q, zero),
                            jnp.where(lane >= HEAD_DIM, q, zero)], axis=0)


def _scores(q2, k):
    return lax.dot_general(q2, k, (((1,), (1,)), ((), ())), preferred_element_type=F32)


def _lane_tile(x, width):
    reps = width // LANES
    return x if reps == 1 else jnp.concatenate([x] * reps, axis=1)


def _lane_fold(x, op):
    out = x[:, :LANES]
    for c in range(LANES, x.shape[1], LANES):
        out = op(out, x[:, c:c + LANES])
    return out


def _rows_tile(ref, j, tk):
    return ref[0, pl.ds(pl.multiple_of(j * tk, tk), tk), :]


def _for_each(n, fn, ways):
    def group(g, c):
        for w in range(ways):
            fn(g * ways + w)
        return c

    def single(j, c):
        fn(j)
        return c

    n_groups = n // ways
    lax.fori_loop(0, n_groups, group, 0)
    lax.fori_loop(n_groups * ways, n, single, 0)


def _merge_halves(y, tq):
    lane = lax.broadcasted_iota(jnp.int32, (tq, LANES), 1)
    return jnp.where(lane < HEAD_DIM, y[:tq], y[tq:])


def _sb_kernel(q_ref, k_ref, v_ref, u_ref, o_ref, z_ref, e_ref, r_ref, acc_ref, *, tq):
    i = pl.program_id(2)
    q2 = _stack_halves(q_ref[0])
    rows = 2 * tq

    def local(t, slot, diagonal):
        j = i - t
        z = z_ref[slot]
        z_ref[1 - slot] = _scores(q2, _rows_tile(k_ref, jnp.maximum(j - 1, 0), tq))
        nz = -z
        log_keep = jnp.minimum(nz, 0.0) - jnp.log(1.0 + jnp.exp2(jnp.minimum(z, nz))) * LOG2E
        if diagonal:
            row = lax.broadcasted_iota(jnp.int32, (rows, tq), 0)
            col = lax.broadcasted_iota(jnp.int32, (rows, tq), 1)
            earlier = col < jnp.where(row >= tq, row - tq, row)
            log_keep = jnp.where(earlier, log_keep, 0.0)
        hi, lo = _split_bf16(log_keep)
        suffix = jnp.dot(jnp.concatenate([hi, lo], axis=1), u_ref[...],
                         preferred_element_type=F32)
        e = z + suffix
        e_ref[j] = jnp.where(earlier, e, NEG) if diagonal else e
        r_ref[j] = jnp.broadcast_to(suffix[:, :1], (rows, LANES))

    z_ref[0] = _scores(q2, _rows_tile(k_ref, i, tq))
    local(0, 0, True)

    def pair(g, c):
        local(2 * g + 1, 1, False)
        local(2 * g + 2, 0, False)
        return c

    lax.fori_loop(0, i // 2, pair, 0)
    pl.when(i % 2 == 1)(lambda: local(i, 1, False))

    def carry_step(jj, carry):
        j = i - jj
        row_sum = r_ref[j]
        r_ref[j] = carry
        return carry + row_sum

    lax.fori_loop(0, i + 1, carry_step, jnp.zeros((rows, LANES), F32))
    acc_ref[...] = jnp.zeros(acc_ref.shape, F32)

    def weigh(j):
        w = jnp.exp2(e_ref[j] + _lane_tile(r_ref[j], tq))
        acc_ref[...] += jnp.dot(w.astype(BF16), _rows_tile(v_ref, j, tq),
                                preferred_element_type=F32)

    _for_each(i + 1, weigh, 2)
    o_ref[0] = _merge_halves(acc_ref[...], tq).astype(o_ref.dtype)


def _diff_kernel(far_ref, q_ref, k_ref, v_ref, near_ref, lam_ref, subln_ref, o_ref,
                 s_ref, m_ref, l_ref, acc_ref, *, tq, lam_init):
    h = pl.program_id(1)
    i = pl.program_id(2)
    q2 = _stack_halves(q_ref[0])
    rows = 2 * tq
    m_ref[...] = jnp.full(m_ref.shape, NEG, F32)

    def score(j, bias):
        s = _scores(q2, _rows_tile(k_ref, j, tq))
        if jnp.ndim(bias) == 2:
            s = (s.reshape(2, tq, tq) + bias[None]).reshape(rows, tq)
        else:
            s = s + bias
        s_ref[j] = s
        m_ref[...] = jnp.maximum(m_ref[...], _lane_fold(s, jnp.maximum))

    score(i, near_ref[0, 0])
    pl.when(i >= 1)(lambda: score(i - 1, near_ref[0, 1]))
    far = far_ref[h]
    _for_each(jnp.maximum(i - 1, 0), lambda j: score(j, far), 4)

    m = jnp.max(m_ref[...], axis=1, keepdims=True)
    m_ref[...] = jnp.broadcast_to(m, m_ref.shape)
    l_ref[...] = jnp.zeros(l_ref.shape, F32)
    acc_ref[...] = jnp.zeros(acc_ref.shape, F32)

    def weigh(j):
        p = jnp.exp2(s_ref[j] - _lane_tile(m_ref[...], tq))
        l_ref[...] += _lane_fold(p, jnp.add)
        acc_ref[...] += jnp.dot(p.astype(BF16), _rows_tile(v_ref, j, tq),
                                preferred_element_type=F32)

    _for_each(i + 1, weigh, 2)

    y = acc_ref[...] / jnp.sum(l_ref[...], axis=1, keepdims=True)
    lmb = lam_ref[...]
    lam = (jnp.exp(jnp.sum(lmb[0:1] * lmb[1:2], axis=-1, keepdims=True))
           - jnp.exp(jnp.sum(lmb[2:3] * lmb[3:4], axis=-1, keepdims=True)) + lam_init)
    y = y[:tq] - lam * y[tq:]
    ms = jnp.mean(y * y, axis=-1, keepdims=True)
    y = y * lax.rsqrt(ms + EPS) * subln_ref[...] * (1.0 - lam_init)
    o_ref[0] = y.astype(o_ref.dtype)


def _ca_kernel(q_ref, k_ref, v_ref, bias_ref, o_ref, *, tq, n_near):
    i = pl.program_id(2)
    q2 = _stack_halves(q_ref[0])
    tiles = []
    for d in range(n_near):
        j = jnp.maximum(i - d, 0)
        s = _scores(q2, _rows_tile(k_ref, j, tq)) + bias_ref[0, d]
        if d > 0:
            s = s + jnp.where(i >= d, 0.0, NEG)
        tiles.append((j, s))
    m = _lane_fold(tiles[0][1], jnp.maximum)
    for _, s in tiles[1:]:
        m = jnp.maximum(m, _lane_fold(s, jnp.maximum))
    m = _lane_tile(jnp.broadcast_to(jnp.max(m, axis=1, keepdims=True), m.shape), tq)
    l = None
    acc = None
    for j, s in tiles:
        p = jnp.exp2(s - m)
        pv = jnp.dot(p.astype(BF16), _rows_tile(v_ref, j, tq), preferred_element_type=F32)
        psum = _lane_fold(p, jnp.add)
        l = psum if l is None else l + psum
        acc = pv if acc is None else acc + pv
    y = acc / jnp.sum(l, axis=1, keepdims=True)
    o_ref[0] = _merge_halves(y, tq).astype(o_ref.dtype)


def _t5_bucket(rel):
    nb = T5_BUCKETS // 2
    max_exact = nb // 2
    ret = jnp.where(rel > 0, nb, 0)
    n = jnp.abs(rel)
    large = max_exact + (jnp.log(jnp.maximum(n, 1).astype(F32) / max_exact)
                         / math.log(T5_MAX_DIST / max_exact) * (nb - max_exact)).astype(jnp.int32)
    large = jnp.minimum(large, nb - 1)
    return ret + jnp.where(n < max_exact, n, large)


def _toeplitz_tile(bias_of_rel, tq, d):
    span = 2 * tq
    x = jnp.arange(span, dtype=jnp.int32)
    x = jnp.where(x < tq, x, x - span)
    vec = jnp.transpose(bias_of_rel(x - d * tq)).astype(F32) * LOG2E
    flat = jnp.tile(vec, (1, tq))[:, :tq * (span - 1)]
    return flat.reshape(vec.shape[0], tq, span - 1)[:, :, :tq]


def _diff_bias_tables(t5_bias, tq):
    assert tq >= T5_MAX_DIST and tq % CHUNK == 0
    r = jnp.arange(tq)[:, None]
    c = jnp.arange(tq)[None, :]
    bias_of_rel = lambda rel: t5_bias[_t5_bucket(rel)]
    diag = jnp.where((c // CHUNK) <= (r // CHUNK), _toeplitz_tile(bias_of_rel, tq, 0), NEG)
    far = t5_bias[_t5_bucket(jnp.int32(-2 * tq))].astype(F32) * LOG2E
    return jnp.stack([diag, _toeplitz_tile(bias_of_rel, tq, 1)], axis=1), far


def _ca_bias_tables(rel_bias, tq):
    assert (CA_LEFT_CHUNKS * CHUNK) % tq == 0 and tq % CHUNK == 0
    n_near = CA_LEFT_CHUNKS * CHUNK // tq + 1
    r = jnp.arange(tq)[:, None]
    c = jnp.arange(tq)[None, :]
    bias_of_rel = lambda rel: rel_bias[jnp.clip(rel, -CA_REL_CLIP, CA_REL_CLIP) + CA_REL_CLIP]
    tiles = []
    for d in range(n_near):
        gap = d * (tq // CHUNK) + r // CHUNK - c // CHUNK
        tiles.append(jnp.where((gap >= 0) & (gap <= CA_LEFT_CHUNKS),
                               _toeplitz_tile(bias_of_rel, tq, d), NEG))
    t = jnp.stack(tiles, axis=1)
    n_heads = t.shape[0]
    t = t.reshape(n_heads // 2, 2, n_near, tq, tq).transpose(0, 2, 1, 3, 4)
    return t.reshape(n_heads // 2, n_near, 2 * tq, tq), n_near


def _attention_calls(proj, batch, t5_bias, diff_lambda, diff_subln, ca_rel_bias, lam_init):
    n_tok, n_proj = proj.shape
    seq = n_tok // batch
    d_model = n_proj // 3
    sb_w = d_model // 4
    proj3 = proj.reshape(batch, seq, n_proj)
    tq = ATT_TILE
    nq = seq // tq
    sb_blocks = sb_w // LANES
    df_heads = (d_model // 2) // LANES
    sb_q0 = 0
    df_q0 = 3 * sb_blocks
    ca_q0 = df_q0 + 3 * df_heads

    def qkv_specs(q0, stride):
        return [pl.BlockSpec((1, tq, LANES), lambda b, h, i: (b, i, q0 + h)),
                pl.BlockSpec((1, seq, LANES), lambda b, h, i: (b, 0, q0 + stride + h)),
                pl.BlockSpec((1, seq, LANES), lambda b, h, i: (b, 0, q0 + 2 * stride + h))]

    out_spec = pl.BlockSpec((1, tq, LANES), lambda b, h, i: (b, i, h))
    stacked = pltpu.VMEM((2 * tq, LANES), F32)

    lower = (jnp.arange(tq)[:, None] >= jnp.arange(tq)[None, :]).astype(BF16)
    u = jnp.concatenate([lower, lower], axis=0)
    ya = pl.pallas_call(
        functools.partial(_sb_kernel, tq=tq),
        grid=(batch, sb_blocks, nq),
        in_specs=qkv_specs(sb_q0, sb_blocks) + [pl.BlockSpec(u.shape, lambda b, h, i: (0, 0))],
        out_specs=out_spec,
        out_shape=jax.ShapeDtypeStruct((batch, seq, sb_w), BF16),
        scratch_shapes=[pltpu.VMEM((2, 2 * tq, tq), F32),
                        pltpu.VMEM((nq, 2 * tq, tq), F32), pltpu.VMEM((nq, 2 * tq, LANES), F32),
                        stacked],
        compiler_params=_cparams(3), name="stick_breaking_attn")(proj3, proj3, proj3, u)

    near, far = _diff_bias_tables(t5_bias, tq)
    yd = pl.pallas_call(
        functools.partial(_diff_kernel, tq=tq, lam_init=lam_init),
        grid=(batch, df_heads, nq),
        in_specs=[pl.BlockSpec(memory_space=pltpu.SMEM)] + qkv_specs(df_q0, df_heads) + [
            pl.BlockSpec((1, 2, tq, tq), lambda b, h, i: (h, 0, 0, 0)),
            pl.BlockSpec((4, HEAD_DIM), lambda b, h, i: (0, 0)),
            pl.BlockSpec((1, LANES), lambda b, h, i: (0, 0))],
        out_specs=out_spec,
        out_shape=jax.ShapeDtypeStruct((batch, seq, d_model // 2), BF16),
        scratch_shapes=[pltpu.VMEM((nq, 2 * tq, tq), F32), stacked, stacked, stacked],
        compiler_params=_cparams(3), name="differential_attn")(
            far, proj3, proj3, proj3, near, diff_lambda.astype(F32),
            diff_subln.astype(F32).reshape(1, LANES))

    ca_bias, n_near = _ca_bias_tables(ca_rel_bias, tq)
    yc = pl.pallas_call(
        functools.partial(_ca_kernel, tq=tq, n_near=n_near),
        grid=(batch, sb_blocks, nq),
        in_specs=qkv_specs(ca_q0, sb_blocks) + [
            pl.BlockSpec((1, n_near, 2 * tq, tq), lambda b, h, i: (h, 0, 0, 0))],
        out_specs=out_spec,
        out_shape=jax.ShapeDtypeStruct((batch, seq, sb_w), BF16),
        compiler_params=_cparams(3), name="chunked_rel_attn")(proj3, proj3, proj3, ca_bias)

    return (ya.reshape(n_tok, sb_w), yd.reshape(n_tok, d_model // 2), yc.reshape(n_tok, sb_w))


def _out_proj_kernel(h_ref, ya_ref, yd_ref, yc_ref, wo_ref, g_ref, rhi_ref, rlo_ref, rb_ref,
                     h2_ref, f_ref, logit_ref):
    wa = ya_ref.shape[1]
    wd = yd_ref.shape[1]
    h2 = (h_ref[...]
          + jnp.dot(ya_ref[...], wo_ref[0:wa, :], preferred_element_type=F32)
          + jnp.dot(yd_ref[...], wo_ref[wa:wa + wd, :], preferred_element_type=F32)
          + jnp.dot(yc_ref[...], wo_ref[wa + wd:, :], preferred_element_type=F32))
    h2_ref[...] = h2
    ms = jnp.mean(h2 * h2, axis=-1, keepdims=True)
    f = h2 * lax.rsqrt(ms + EPS) * g_ref[...]
    f_hi, f_lo = _split_bf16(f)
    f_ref[...] = f_hi
    logit_ref[...] = (jnp.dot(f_hi, rhi_ref[...], preferred_element_type=F32)
                      + jnp.dot(f_lo, rhi_ref[...], preferred_element_type=F32)
                      + jnp.dot(f_hi, rlo_ref[...], preferred_element_type=F32)
                      + rb_ref[...])


def _out_proj_call(h, ya, yd, yc, wo_bf, gain, r_hi, r_lo, r_bias):
    n_tok, d_model = h.shape
    tm = ROW_TILE
    row = lambda w: pl.BlockSpec((tm, w), lambda i: (i, 0))
    full = lambda a: pl.BlockSpec(a.shape, lambda i: (0, 0))
    gain = gain.reshape(1, d_model)
    return pl.pallas_call(
        _out_proj_kernel,
        grid=(n_tok // tm,),
        in_specs=[row(d_model), row(ya.shape[1]), row(yd.shape[1]), row(yc.shape[1]),
                  full(wo_bf), full(gain), full(r_hi), full(r_lo), full(r_bias)],
        out_specs=[row(d_model), row(d_model), row(LANES)],
        out_shape=[jax.ShapeDtypeStruct((n_tok, d_model), F32),
                   jax.ShapeDtypeStruct((n_tok, d_model), BF16),
                   jax.ShapeDtypeStruct((n_tok, LANES), F32)],
        compiler_params=_cparams(1), name="out_proj_router")(
            h, ya, yd, yc, wo_bf, gain, r_hi, r_lo, r_bias)


def _expert_kernel(blk_expert_ref, x_ref, gate_ref, wg_ref, wu_ref, wd_ref, y_ref):
    del blk_expert_ref
    x = x_ref[...]
    g = jnp.dot(x, wg_ref[0], preferred_element_type=F32)
    u = jnp.dot(x, wu_ref[0], preferred_element_type=F32)
    hid = (g / (1.0 + jnp.exp(-g)) * u).astype(BF16)
    y = jnp.dot(hid, wd_ref[0], preferred_element_type=F32)
    y_ref[...] = y * _lane_tile(gate_ref[...], y.shape[1])


def _expert_call(xs, row_gate, blk_expert, wg_bf, wu_bf, wd_bf):
    n_rows, d_model = xs.shape
    d_exp = wg_bf.shape[2]
    tb = MOE_TILE
    grid_spec = pltpu.PrefetchScalarGridSpec(
        num_scalar_prefetch=1,
        grid=(n_rows // tb,),
        in_specs=[pl.BlockSpec((tb, d_model), lambda i, be: (i, 0)),
                  pl.BlockSpec((tb, LANES), lambda i, be: (i, 0)),
                  pl.BlockSpec((1, d_model, d_exp), lambda i, be: (be[i], 0, 0)),
                  pl.BlockSpec((1, d_model, d_exp), lambda i, be: (be[i], 0, 0)),
                  pl.BlockSpec((1, d_exp, d_model), lambda i, be: (be[i], 0, 0))],
        out_specs=pl.BlockSpec((tb, d_model), lambda i, be: (i, 0)))
    return pl.pallas_call(
        _expert_kernel, grid_spec=grid_spec,
        out_shape=jax.ShapeDtypeStruct((n_rows, d_model), F32),
        compiler_params=_cparams(1), name="expert_mlp")(
            blk_expert, xs, row_gate, wg_bf, wu_bf, wd_bf)


def _route(logits, tb):
    n_tok = logits.shape[0]
    tok_idx = jnp.arange(n_tok)
    g_logits = logits[:, :N_GROUPS]
    g_sel = jnp.argmax(g_logits, axis=-1)
    g_w = jax.nn.softmax(g_logits, axis=-1)[tok_idx, g_sel]
    e_logits = logits[:, N_GROUPS:N_GROUPS + N_EXPERTS].reshape(n_tok, N_GROUPS, EXPERTS_PER_GROUP)
    e_prob = jax.nn.softmax(e_logits[tok_idx, g_sel], axis=-1)
    top_p, top_i = lax.top_k(e_prob, TOP_K_INNER)
    top_p = top_p / jnp.sum(top_p, axis=-1, keepdims=True)
    eid = (g_sel[:, None] * EXPERTS_PER_GROUP + top_i).reshape(-1).astype(jnp.int32)
    gate = (g_w[:, None] * top_p).reshape(-1)
    n_asg = n_tok * TOP_K_INNER
    order = jnp.argsort(eid).astype(jnp.int32)
    eid_s = eid[order]
    counts = jnp.zeros((N_EXPERTS,), jnp.int32).at[eid].add(1)
    padded = (counts + tb - 1) // tb * tb
    start = jnp.cumsum(counts) - counts
    pend = jnp.cumsum(padded)
    pstart = pend - padded
    dest_s = pstart[eid_s] + (jnp.arange(n_asg, dtype=jnp.int32) - start[eid_s])
    n_rows = n_asg + N_EXPERTS * tb
    n_blocks = n_rows // tb
    row_asg = jnp.full((n_rows,), n_asg, jnp.int32).at[dest_s].set(order)
    valid = row_asg < n_asg
    safe_asg = jnp.where(valid, row_asg, 0)
    row_tok = safe_asg // TOP_K_INNER
    row_gate = jnp.where(valid, gate[safe_asg], 0.0)
    asg_row = jnp.zeros((n_asg,), jnp.int32).at[order].set(dest_s)
    blk_expert = jnp.minimum(
        jnp.searchsorted(pend, jnp.arange(n_blocks, dtype=jnp.int32) * tb, side="right"),
        N_EXPERTS - 1).astype(jnp.int32)
    return row_tok, row_gate, asg_row, blk_expert


def _moe(f_bf, logits, wg_bf, wu_bf, wd_bf):
    n_tok, d_model = f_bf.shape
    row_tok, row_gate, asg_row, blk_expert = _route(logits, MOE_TILE)
    xs = f_bf[row_tok]
    gate_b = jnp.broadcast_to(row_gate[:, None], (row_gate.shape[0], LANES))
    y_rows = _expert_call(xs, gate_b, blk_expert, wg_bf, wu_bf, wd_bf)
    return y_rows[asg_row].reshape(n_tok, TOP_K_INNER * d_model)


def _combine_kernel(h_ref, y0_ref, y1_ref, o_ref):
    o_ref[...] = h_ref[...] + y0_ref[...] + y1_ref[...]


def _combine_call(h2, y_pair):
    n_tok, d_model = h2.shape
    tm = ROW_TILE
    return pl.pallas_call(
        _combine_kernel, grid=(n_tok // tm,),
        in_specs=[pl.BlockSpec((tm, d_model), lambda i: (i, 0)),
                  pl.BlockSpec((tm, d_model), lambda i: (i, 0)),
                  pl.BlockSpec((tm, d_model), lambda i: (i, 1))],
        out_specs=pl.BlockSpec((tm, d_model), lambda i: (i, 0)),
        out_shape=jax.ShapeDtypeStruct((n_tok, d_model), F32),
        compiler_params=_cparams(1), name="moe_combine")(h2, y_pair, y_pair)


def _proj_col_gain(diff_qk_norm, ca_qk_norm, d_model):
    sb_w, df_w = d_model // 4, d_model // 2
    ones = lambda n: jnp.ones((n,), F32)
    rep = lambda g, n: jnp.tile(g.astype(F32), n // HEAD_DIM)
    return jnp.concatenate([
        ones(3 * sb_w),
        rep(diff_qk_norm[0], df_w) * Q_FOLD, rep(diff_qk_norm[1], df_w), ones(df_w),
        rep(ca_qk_norm[0], sb_w) * Q_FOLD, rep(ca_qk_norm[1], sb_w), ones(sb_w)]).reshape(1, -1)


def kernel(x, t5_bias, attn_norm, w_in, diff_qk_norm, diff_lambda, diff_subln, ca_qk_norm,
           ca_rel_bias, w_out, ffn_norm, router_group_w, router_group_b, router_expert_w,
           router_expert_b, expert_w_gate, expert_w_up, expert_w_down):
    batch, seq, d_model = x.shape
    depth = w_in.shape[0]
    n_tok = batch * seq
    lane_group = jnp.arange(LANES) // HEAD_DIM
    group_mean = ((lane_group[:, None] == lane_group[None, :]).astype(F32) / HEAD_DIM).astype(BF16)

    h = x.reshape(n_tok, d_model)
    y_pair = None
    for l in range(depth):
        lam_init = 0.8 - 0.6 * math.exp(-0.3 * l)
        col_gain = _proj_col_gain(diff_qk_norm[l], ca_qk_norm[l], d_model)
        h, proj = _proj_call(h, y_pair, attn_norm[l], w_in[l].astype(BF16), col_gain, group_mean)
        ya, yd, yc = _attention_calls(proj, batch, t5_bias, diff_lambda[l], diff_subln[l],
                                      ca_rel_bias[l], lam_init)
        r_w = jnp.concatenate([router_group_w[l], router_expert_w[l]], axis=1).astype(F32)
        r_w = jnp.pad(r_w, ((0, 0), (0, LANES - r_w.shape[1])))
        r_hi, r_lo = _split_bf16(r_w)
        r_b = jnp.concatenate([router_group_b[l], router_expert_b[l]]).astype(F32)
        r_b = jnp.pad(r_b, (0, LANES - r_b.shape[0])).reshape(1, LANES)
        h, f_bf, logits = _out_proj_call(h, ya, yd, yc, w_out[l].astype(BF16), ffn_norm[l],
                                         r_hi, r_lo, r_b)
        y_pair = _moe(f_bf, logits, expert_w_gate[l].astype(BF16), expert_w_up[l].astype(BF16),
                      expert_w_down[l].astype(BF16))
    return _combine_call(h, y_pair).reshape(batch, seq, d_model)
```

```python
import functools
import math

import jax
import jax.numpy as jnp
from jax import lax
from jax.experimental import pallas as pl
from jax.experimental.pallas import tpu as pltpu

F32 = jnp.float32
BF16 = jnp.bfloat16

LANES = 128
HEAD_DIM = 64
CHUNK = 64
CA_LEFT_CHUNKS = 8
CA_REL_CLIP = 128
T5_BUCKETS = 32
T5_MAX_DIST = 256
N_GROUPS = 4
EXPERTS_PER_GROUP = 8
N_EXPERTS = N_GROUPS * EXPERTS_PER_GROUP
TOP_K_INNER = 2
EPS = 1e-6
NEG = -1e30
LOG2E = math.log2(math.e)
Q_FOLD = HEAD_DIM ** -0.5 * LOG2E

ROW_TILE = 512
ATT_TILE = 256
MOE_TILE = 256
VMEM_LIMIT = 48 * 1024 * 1024


def _cparams(n_axes):
    return pltpu.CompilerParams(dimension_semantics=("arbitrary",) * n_axes,
                                vmem_limit_bytes=VMEM_LIMIT)


def _split_bf16(x):
    hi = x.astype(BF16)
    lo = (x - hi.astype(F32)).astype(BF16)
    return hi, lo


def _col_kind(col, d_model):
    sb_w, df_w = d_model // 4, d_model // 2
    sb_end = 3 * sb_w
    df_end = sb_end + 3 * df_w
    if col < sb_w:
        return "scale"
    if col < sb_end:
        return "plain"
    if col < sb_end + 2 * df_w:
        return "norm"
    if col < df_end:
        return "plain"
    if col < df_end + 2 * sb_w:
        return "norm"
    return "plain"


def _proj_kernel(h_ref, g_ref, w_ref, cg_ref, gm_ref, o_ref, *, d_model, col_chunk):
    x = h_ref[...]
    ms = jnp.mean(x * x, axis=-1, keepdims=True)
    xn = (x * lax.rsqrt(ms + EPS) * g_ref[...]).astype(BF16)
    n_out = w_ref.shape[1]
    gm = gm_ref[...]
    for c0 in range(0, n_out, col_chunk):
        acc = jnp.dot(xn, w_ref[:, c0:c0 + col_chunk], preferred_element_type=F32)
        for s0 in range(0, col_chunk, LANES):
            col = c0 + s0
            blk = acc[:, s0:s0 + LANES]
            kind = _col_kind(col, d_model)
            if kind == "norm":
                hi, lo = _split_bf16(blk * blk)
                msq = (jnp.dot(hi, gm, preferred_element_type=F32)
                       + jnp.dot(lo, gm, preferred_element_type=F32))
                blk = blk * lax.rsqrt(msq + EPS) * cg_ref[:, col:col + LANES]
            elif kind == "scale":
                blk = blk * Q_FOLD
            o_ref[:, col:col + LANES] = blk.astype(BF16)


def _proj_call(h, gain, w_bf, col_gain, group_mean):
    n_tok, d_model = h.shape
    n_out = w_bf.shape[1]
    tm = ROW_TILE
    return pl.pallas_call(
        functools.partial(_proj_kernel, d_model=d_model, col_chunk=512),
        grid=(n_tok // tm,),
        in_specs=[pl.BlockSpec((tm, d_model), lambda i: (i, 0)),
                  pl.BlockSpec((1, d_model), lambda i: (0, 0)),
                  pl.BlockSpec((d_model, n_out), lambda i: (0, 0)),
                  pl.BlockSpec((1, n_out), lambda i: (0, 0)),
                  pl.BlockSpec((LANES, LANES), lambda i: (0, 0))],
        out_specs=pl.BlockSpec((tm, n_out), lambda i: (i, 0)),
        out_shape=jax.ShapeDtypeStruct((n_tok, n_out), BF16),
        compiler_params=_cparams(1), name="norm_in_proj")(
            h, gain.reshape(1, d_model), w_bf, col_gain, group_mean)


def _stack_halves(q):
    lane = lax.broadcasted_iota(jnp.int32, q.shape, 1)
    zero = jnp.zeros_like(q)
    return jnp.concatenate([jnp.where(lane < HEAD_DIM, q, zero),
                            jnp.where(lane >= HEAD_DIM, q, zero)], axis=0)


def _scores(q2, k):
    return lax.dot_general(q2, k, (((1,), (1,)), ((), ())), preferred_element_type=F32)


def _lane_tile(x, width):
    reps = width // LANES
    return x if reps == 1 else jnp.concatenate([x] * reps, axis=1)


def _lane_fold(x, op):
    out = x[:, :LANES]
    for c in range(LANES, x.shape[1], LANES):
        out = op(out, x[:, c:c + LANES])
    return out


def _rows_tile(ref, j, tk):
    return ref[0, pl.ds(pl.multiple_of(j * tk, tk), tk), :]


def _for_each(n, fn, ways):
    def group(g, c):
        for w in range(ways):
            fn(g * ways + w)
        return c

    def single(j, c):
        fn(j)
        return c

    n_groups = n // ways
    lax.fori_loop(0, n_groups, group, 0)
    lax.fori_loop(n_groups * ways, n, single, 0)


def _merge_halves(y, tq):
    lane = lax.broadcasted_iota(jnp.int32, (tq, LANES), 1)
    return jnp.where(lane < HEAD_DIM, y[:tq], y[tq:])


def _sb_kernel(q_ref, k_ref, v_ref, u_ref, o_ref, z_ref, e_ref, r_ref, acc_ref, *, tq):
    i = pl.program_id(2)
    q2 = _stack_halves(q_ref[0])
    rows = 2 * tq

    def local(t, slot, diagonal):
        j = i - t
        z = z_ref[slot]
        z_ref[1 - slot] = _scores(q2, _rows_tile(k_ref, jnp.maximum(j - 1, 0), tq))
        nz = -z
        log_keep = jnp.minimum(nz, 0.0) - jnp.log(1.0 + jnp.exp2(jnp.minimum(z, nz))) * LOG2E
        if diagonal:
            row = lax.broadcasted_iota(jnp.int32, (rows, tq), 0)
            col = lax.broadcasted_iota(jnp.int32, (rows, tq), 1)
            earlier = col < jnp.where(row >= tq, row - tq, row)
            log_keep = jnp.where(earlier, log_keep, 0.0)
        hi, lo = _split_bf16(log_keep)
        suffix = jnp.dot(jnp.concatenate([hi, lo], axis=1), u_ref[...],
                         preferred_element_type=F32)
        e = z + suffix
        e_ref[j] = jnp.where(earlier, e, NEG) if diagonal else e
        r_ref[j] = jnp.broadcast_to(suffix[:, :1], (rows, LANES))

    z_ref[0] = _scores(q2, _rows_tile(k_ref, i, tq))
    local(0, 0, True)

    def pair(g, c):
        local(2 * g + 1, 1, False)
        local(2 * g + 2, 0, False)
        return c

    lax.fori_loop(0, i // 2, pair, 0)
    pl.when(i % 2 == 1)(lambda: local(i, 1, False))

    def carry_step(jj, carry):
        j = i - jj
        row_sum = r_ref[j]
        r_ref[j] = carry
        return carry + row_sum

    lax.fori_loop(0, i + 1, carry_step, jnp.zeros((rows, LANES), F32))
    acc_ref[...] = jnp.zeros(acc_ref.shape, F32)

    def weigh(j):
        w = jnp.exp2(e_ref[j] + _lane_tile(r_ref[j], tq))
        acc_ref[...] += jnp.dot(w.astype(BF16), _rows_tile(v_ref, j, tq),
                                preferred_element_type=F32)

    _for_each(i + 1, weigh, 2)
    o_ref[0] = _merge_halves(acc_ref[...], tq).astype(o_ref.dtype)


def _diff_kernel(far_ref, q_ref, k_ref, v_ref, near_ref, lam_ref, subln_ref, o_ref,
                 s_ref, m_ref, l_ref, acc_ref, *, tq, lam_init):
    h = pl.program_id(1)
    i = pl.program_id(2)
    q2 = _stack_halves(q_ref[0])
    rows = 2 * tq
    m_ref[...] = jnp.full(m_ref.shape, NEG, F32)

    def score(j, bias):
        s = _scores(q2, _rows_tile(k_ref, j, tq))
        if jnp.ndim(bias) == 2:
            s = (s.reshape(2, tq, tq) + bias[None]).reshape(rows, tq)
        else:
            s = s + bias
        s_ref[j] = s
        m_ref[...] = jnp.maximum(m_ref[...], _lane_fold(s, jnp.maximum))

    score(i, near_ref[0, 0])
    pl.when(i >= 1)(lambda: score(i - 1, near_ref[0, 1]))
    far = far_ref[h]
    _for_each(jnp.maximum(i - 1, 0), lambda j: score(j, far), 4)

    m = jnp.max(m_ref[...], axis=1, keepdims=True)
    m_ref[...] = jnp.broadcast_to(m, m_ref.shape)
    l_ref[...] = jnp.zeros(l_ref.shape, F32)
    acc_ref[...] = jnp.zeros(acc_ref.shape, F32)

    def weigh(j):
        p = jnp.exp2(s_ref[j] - _lane_tile(m_ref[...], tq))
        l_ref[...] += _lane_fold(p, jnp.add)
        acc_ref[...] += jnp.dot(p.astype(BF16), _rows_tile(v_ref, j, tq),
                                preferred_element_type=F32)

    _for_each(i + 1, weigh, 2)

    y = acc_ref[...] / jnp.sum(l_ref[...], axis=1, keepdims=True)
    lmb = lam_ref[...]
    lam = (jnp.exp(jnp.sum(lmb[0:1] * lmb[1:2], axis=-1, keepdims=True))
           - jnp.exp(jnp.sum(lmb[2:3] * lmb[3:4], axis=-1, keepdims=True)) + lam_init)
    y = y[:tq] - lam * y[tq:]
    ms = jnp.mean(y * y, axis=-1, keepdims=True)
    y = y * lax.rsqrt(ms + EPS) * subln_ref[...] * (1.0 - lam_init)
    o_ref[0] = y.astype(o_ref.dtype)


def _ca_kernel(q_ref, k_ref, v_ref, bias_ref, o_ref, *, tq, n_near):
    i = pl.program_id(2)
    q2 = _stack_halves(q_ref[0])
    tiles = []
    for d in range(n_near):
        j = jnp.maximum(i - d, 0)
        s = _scores(q2, _rows_tile(k_ref, j, tq)) + bias_ref[0, d]
        if d > 0:
            s = s + jnp.where(i >= d, 0.0, NEG)
        tiles.append((j, s))
    m = _lane_fold(tiles[0][1], jnp.maximum)
    for _, s in tiles[1:]:
        m = jnp.maximum(m, _lane_fold(s, jnp.maximum))
    m = _lane_tile(jnp.broadcast_to(jnp.max(m, axis=1, keepdims=True), m.shape), tq)
    l = None
    acc = None
    for j, s in tiles:
        p = jnp.exp2(s - m)
        pv = jnp.dot(p.astype(BF16), _rows_tile(v_ref, j, tq), preferred_element_type=F32)
        psum = _lane_fold(p, jnp.add)
        l = psum if l is None else l + psum
        acc = pv if acc is None else acc + pv
    y = acc / jnp.sum(l, axis=1, keepdims=True)
    o_ref[0] = _merge_halves(y, tq).astype(o_ref.dtype)


def _t5_bucket(rel):
    nb = T5_BUCKETS // 2
    max_exact = nb // 2
    ret = jnp.where(rel > 0, nb, 0)
    n = jnp.abs(rel)
    large = max_exact + (jnp.log(jnp.maximum(n, 1).astype(F32) / max_exact)
                         / math.log(T5_MAX_DIST / max_exact) * (nb - max_exact)).astype(jnp.int32)
    large = jnp.minimum(large, nb - 1)
    return ret + jnp.where(n < max_exact, n, large)


def _toeplitz_tile(bias_of_rel, tq, d):
    span = 2 * tq
    x = jnp.arange(span, dtype=jnp.int32)
    x = jnp.where(x < tq, x, x - span)
    vec = jnp.transpose(bias_of_rel(x - d * tq)).astype(F32) * LOG2E
    flat = jnp.tile(vec, (1, tq))[:, :tq * (span - 1)]
    return flat.reshape(vec.shape[0], tq, span - 1)[:, :, :tq]


def _diff_bias_tables(t5_bias, tq):
    assert tq >= T5_MAX_DIST and tq % CHUNK == 0
    r = jnp.arange(tq)[:, None]
    c = jnp.arange(tq)[None, :]
    bias_of_rel = lambda rel: t5_bias[_t5_bucket(rel)]
    diag = jnp.where((c // CHUNK) <= (r // CHUNK), _toeplitz_tile(bias_of_rel, tq, 0), NEG)
    far = t5_bias[_t5_bucket(jnp.int32(-2 * tq))].astype(F32) * LOG2E
    return jnp.stack([diag, _toeplitz_tile(bias_of_rel, tq, 1)], axis=1), far


def _ca_bias_tables(rel_bias, tq):
    assert (CA_LEFT_CHUNKS * CHUNK) % tq == 0 and tq % CHUNK == 0
    n_near = CA_LEFT_CHUNKS * CHUNK // tq + 1
    r = jnp.arange(tq)[:, None]
    c = jnp.arange(tq)[None, :]
    bias_of_rel = lambda rel: rel_bias[jnp.clip(rel, -CA_REL_CLIP, CA_REL_CLIP) + CA_REL_CLIP]
    tiles = []
    for d in range(n_near):
        gap = d * (tq // CHUNK) + r // CHUNK - c // CHUNK
        tiles.append(jnp.where((gap >= 0) & (gap <= CA_LEFT_CHUNKS),
                               _toeplitz_tile(bias_of_rel, tq, d), NEG))
    t = jnp.stack(tiles, axis=1)
    n_heads = t.shape[0]
    t = t.reshape(n_heads // 2, 2, n_near, tq, tq).transpose(0, 2, 1, 3, 4)
    return t.reshape(n_heads // 2, n_near, 2 * tq, tq), n_near


def _attention_calls(proj, batch, t5_bias, diff_lambda, diff_subln, ca_rel_bias, lam_init):
    n_tok, n_proj = proj.shape
    seq = n_tok // batch
    d_model = n_proj // 3
    sb_w = d_model // 4
    proj3 = proj.reshape(batch, seq, n_proj)
    tq = ATT_TILE
    nq = seq // tq
    sb_blocks = sb_w // LANES
    df_heads = (d_model // 2) // LANES
    sb_q0 = 0
    df_q0 = 3 * sb_blocks
    ca_q0 = df_q0 + 3 * df_heads

    def qkv_specs(q0, stride):
        return [pl.BlockSpec((1, tq, LANES), lambda b, h, i: (b, i, q0 + h)),
                pl.BlockSpec((1, seq, LANES), lambda b, h, i: (b, 0, q0 + stride + h)),
                pl.BlockSpec((1, seq, LANES), lambda b, h, i: (b, 0, q0 + 2 * stride + h))]

    out_spec = pl.BlockSpec((1, tq, LANES), lambda b, h, i: (b, i, h))
    stacked = pltpu.VMEM((2 * tq, LANES), F32)

    lower = (jnp.arange(tq)[:, None] >= jnp.arange(tq)[None, :]).astype(BF16)
    u = jnp.concatenate([lower, lower], axis=0)
    ya = pl.pallas_call(
        functools.partial(_sb_kernel, tq=tq),
        grid=(batch, sb_blocks, nq),
        in_specs=qkv_specs(sb_q0, sb_blocks) + [pl.BlockSpec(u.shape, lambda b, h, i: (0, 0))],
        out_specs=out_spec,
        out_shape=jax.ShapeDtypeStruct((batch, seq, sb_w), BF16),
        scratch_shapes=[pltpu.VMEM((2, 2 * tq, tq), F32),
                        pltpu.VMEM((nq, 2 * tq, tq), F32), pltpu.VMEM((nq, 2 * tq, LANES), F32),
                        stacked],
        compiler_params=_cparams(3), name="stick_breaking_attn")(proj3, proj3, proj3, u)

    near, far = _diff_bias_tables(t5_bias, tq)
    yd = pl.pallas_call(
        functools.partial(_diff_kernel, tq=tq, lam_init=lam_init),
        grid=(batch, df_heads, nq),
        in_specs=[pl.BlockSpec(memory_space=pltpu.SMEM)] + qkv_specs(df_q0, df_heads) + [
            pl.BlockSpec((1, 2, tq, tq), lambda b, h, i: (h, 0, 0, 0)),
            pl.BlockSpec((4, HEAD_DIM), lambda b, h, i: (0, 0)),
            pl.BlockSpec((1, LANES), lambda b, h, i: (0, 0))],
        out_specs=out_spec,
        out_shape=jax.ShapeDtypeStruct((batch, seq, d_model // 2), BF16),
        scratch_shapes=[pltpu.VMEM((nq, 2 * tq, tq), F32), stacked, stacked, stacked],
        compiler_params=_cparams(3), name="differential_attn")(
            far, proj3, proj3, proj3, near, diff_lambda.astype(F32),
            diff_subln.astype(F32).reshape(1, LANES))

    ca_bias, n_near = _ca_bias_tables(ca_rel_bias, tq)
    yc = pl.pallas_call(
        functools.partial(_ca_kernel, tq=tq, n_near=n_near),
        grid=(batch, sb_blocks, nq),
        in_specs=qkv_specs(ca_q0, sb_blocks) + [
            pl.BlockSpec((1, n_near, 2 * tq, tq), lambda b, h, i: (h, 0, 0, 0))],
        out_specs=out_spec,
        out_shape=jax.ShapeDtypeStruct((batch, seq, sb_w), BF16),
        compiler_params=_cparams(3), name="chunked_rel_attn")(proj3, proj3, proj3, ca_bias)

    return (ya.reshape(n_tok, sb_w), yd.reshape(n_tok, d_model // 2), yc.reshape(n_tok, sb_w))


META_E, META_GATE, META_RANK = 0, 2, 4


def _row_min_lane(mask, lane_f):
    return jnp.min(jnp.where(mask, lane_f, float(LANES)), axis=1, keepdims=True)


def _out_proj_kernel(h_ref, ya_ref, yd_ref, yc_ref, wo_ref, g_ref, rhi_ref, rlo_ref, rb_ref,
                     ltri_ref, h2_ref, f_ref, meta_ref, count_ref, run_ref):
    wa = ya_ref.shape[1]
    wd = yd_ref.shape[1]
    h2 = (h_ref[...]
          + jnp.dot(ya_ref[...], wo_ref[0:wa, :], preferred_element_type=F32)
          + jnp.dot(yd_ref[...], wo_ref[wa:wa + wd, :], preferred_element_type=F32)
          + jnp.dot(yc_ref[...], wo_ref[wa + wd:, :], preferred_element_type=F32))
    h2_ref[...] = h2
    ms = jnp.mean(h2 * h2, axis=-1, keepdims=True)
    f = h2 * lax.rsqrt(ms + EPS) * g_ref[...]
    f_ref[...] = f
    f_hi, f_lo = _split_bf16(f)
    logits = (jnp.dot(f_hi, rhi_ref[...], preferred_element_type=F32)
              + jnp.dot(f_lo, rhi_ref[...], preferred_element_type=F32)
              + jnp.dot(f_hi, rlo_ref[...], preferred_element_type=F32)
              + rb_ref[...])

    lane = lax.broadcasted_iota(jnp.int32, logits.shape, 1)
    lane_f = lane.astype(F32)
    is_group = lane < N_GROUPS
    gl = jnp.where(is_group, logits, NEG)
    g_max = jnp.max(gl, axis=1, keepdims=True)
    g_sel = _row_min_lane(gl == g_max, lane_f)
    g_w = 1.0 / jnp.sum(jnp.where(is_group, jnp.exp(logits - g_max), 0.0), axis=1, keepdims=True)
    first_lane = float(N_GROUPS) + float(EXPERTS_PER_GROUP) * g_sel
    in_group = (lane_f >= first_lane) & (lane_f < first_lane + float(EXPERTS_PER_GROUP))
    el = jnp.where(in_group, logits, NEG)
    m1 = jnp.max(el, axis=1, keepdims=True)
    i1 = _row_min_lane(el == m1, lane_f)
    el = jnp.where(lane_f == i1, NEG, el)
    m2 = jnp.max(el, axis=1, keepdims=True)
    i2 = _row_min_lane(el == m2, lane_f)
    r = jnp.exp(m2 - m1)
    p1 = 1.0 / (1.0 + r)
    gate1 = g_w * p1
    gate2 = g_w * (r * p1)

    @pl.when(pl.program_id(0) == 0)
    def _():
        run_ref[...] = jnp.zeros(run_ref.shape, F32)

    hit1 = lane_f == i1
    hit2 = lane_f == i2
    one_hot = jnp.where(hit1 | hit2, 1.0, 0.0)
    pos = run_ref[...] + jnp.dot(ltri_ref[...], one_hot.astype(BF16), preferred_element_type=F32)
    rank1 = jnp.sum(jnp.where(hit1, pos, 0.0), axis=1, keepdims=True)
    rank2 = jnp.sum(jnp.where(hit2, pos, 0.0), axis=1, keepdims=True)
    run_ref[...] += jnp.sum(one_hot, axis=0, keepdims=True)
    count_ref[...] = run_ref[...]

    meta = jnp.zeros(logits.shape, F32)
    for at, val in ((META_E, i1 - N_GROUPS), (META_E + 1, i2 - N_GROUPS), (META_GATE, gate1),
                    (META_GATE + 1, gate2), (META_RANK, rank1), (META_RANK + 1, rank2)):
        meta = jnp.where(lane == at, val, meta)
    meta_ref[...] = meta


def _out_proj_call(h, ya, yd, yc, wo_bf, gain, r_hi, r_lo, r_bias):
    n_tok, d_model = h.shape
    tm = ROW_TILE
    row = lambda w: pl.BlockSpec((tm, w), lambda i: (i, 0))
    full = lambda a: pl.BlockSpec(a.shape, lambda i: (0, 0))
    gain = gain.reshape(1, d_model)
    ltri = (jnp.arange(tm)[:, None] > jnp.arange(tm)[None, :]).astype(BF16)
    return pl.pallas_call(
        _out_proj_kernel,
        grid=(n_tok // tm,),
        in_specs=[row(d_model), row(ya.shape[1]), row(yd.shape[1]), row(yc.shape[1]),
                  full(wo_bf), full(gain), full(r_hi), full(r_lo), full(r_bias), full(ltri)],
        out_specs=[row(d_model), row(d_model), row(LANES),
                   pl.BlockSpec((1, LANES), lambda i: (0, 0))],
        out_shape=[jax.ShapeDtypeStruct((n_tok, d_model), F32),
                   jax.ShapeDtypeStruct((n_tok, d_model), F32),
                   jax.ShapeDtypeStruct((n_tok, LANES), F32),
                   jax.ShapeDtypeStruct((1, LANES), F32)],
        scratch_shapes=[pltpu.VMEM((1, LANES), F32)],
        compiler_params=_cparams(1), name="out_proj_router")(
            h, ya, yd, yc, wo_bf, gain, r_hi, r_lo, r_bias, ltri)


def _dispatch_plan(meta, count_row, tb, tm):
    n_tok = meta.shape[0]
    counts = count_row[0, N_GROUPS:N_GROUPS + N_EXPERTS].astype(jnp.int32)
    padded = (counts + tb - 1) // tb * tb
    e = jnp.arange(N_EXPERTS)
    pend = jnp.sum(jnp.where(e[:, None] >= e[None, :], padded[None, :], 0), axis=1)
    pstart = pend - padded
    n_rows = n_tok * TOP_K_INNER + N_EXPERTS * tb
    block_start = jnp.arange(n_rows // tb, dtype=jnp.int32) * tb
    blk_expert = jnp.minimum(jnp.sum((pend[None, :] <= block_start[:, None]).astype(jnp.int32), axis=1),
                             N_EXPERTS - 1)
    n_used = pend[-1:] // tb
    cols = [META_E, META_E + 1, META_RANK, META_RANK + 1]
    idx = jnp.stack([meta[:, c] for c in cols]).astype(jnp.int32)
    idx = idx.reshape(len(cols), n_tok // tm, tm).transpose(1, 0, 2)
    return pstart.astype(jnp.int32), pend.astype(jnp.int32), blk_expert, n_used, idx, n_rows


def _assignment_row(pstart_ref, idx_ref, r, k):
    return pstart_ref[idx_ref[0, k, r]] + idx_ref[0, TOP_K_INNER + k, r]


def _for_each_assignment(tm, fn):
    def body(r, c):
        for k in range(TOP_K_INNER):
            fn(r, k)
        return c

    lax.fori_loop(0, tm, body, 0, unroll=8)


def _dispatch_kernel(pstart_ref, pend_ref, idx_ref, f_ref, xs_ref, zero_ref, row_sem, zero_sem,
                     *, tb):
    tm = f_ref.shape[0]

    @pl.when(pl.program_id(0) == 0)
    def _():
        zero_ref[...] = jnp.zeros(zero_ref.shape, F32)

        def block_copy(start):
            return pltpu.make_async_copy(zero_ref, xs_ref.at[pl.ds(pl.multiple_of(start, tb), tb)],
                                         zero_sem)

        for e in range(N_EXPERTS):
            block_copy(jnp.maximum(pend_ref[e] - tb, 0)).start()
        for e in range(N_EXPERTS):
            block_copy(jnp.maximum(pend_ref[e] - tb, 0)).wait()

        def fill(b, c):
            block_copy(b * tb).start()
            block_copy(b * tb).wait()
            return c

        lax.fori_loop(pend_ref[N_EXPERTS - 1] // tb, xs_ref.shape[0] // tb, fill, 0)

    def row_copy(r, k):
        dest = _assignment_row(pstart_ref, idx_ref, r, k)
        return pltpu.make_async_copy(f_ref.at[pl.ds(r, 1)], xs_ref.at[pl.ds(dest, 1)], row_sem)

    _for_each_assignment(tm, lambda r, k: row_copy(r, k).start())
    _for_each_assignment(tm, lambda r, k: row_copy(r, k).wait())


def _dispatch_call(f, pstart, pend, idx, n_rows):
    n_tok, d_model = f.shape
    tm = idx.shape[2]
    tb = MOE_TILE
    grid_spec = pltpu.PrefetchScalarGridSpec(
        num_scalar_prefetch=2,
        grid=(n_tok // tm,),
        in_specs=[pl.BlockSpec((1,) + idx.shape[1:], lambda i, ps, pe: (i, 0, 0),
                               memory_space=pltpu.SMEM),
                  pl.BlockSpec((tm, d_model), lambda i, ps, pe: (i, 0))],
        out_specs=pl.BlockSpec(memory_space=pl.ANY),
        scratch_shapes=[pltpu.VMEM((tb, d_model), F32), pltpu.SemaphoreType.DMA(()),
                        pltpu.SemaphoreType.DMA(())])
    return pl.pallas_call(
        functools.partial(_dispatch_kernel, tb=tb), grid_spec=grid_spec,
        out_shape=jax.ShapeDtypeStruct((n_rows, d_model), F32),
        compiler_params=_cparams(1), name="moe_dispatch")(pstart, pend, idx, f)


def _expert_kernel(blk_expert_ref, n_used_ref, x_ref, wg_ref, wu_ref, wd_ref, y_ref,
                   wg_bf, wu_bf, wd_bf):
    i = pl.program_id(0)
    used = i < n_used_ref[0]
    e = blk_expert_ref[i]
    e_prev = blk_expert_ref[jnp.maximum(i - 1, 0)]

    @pl.when(used & ((i == 0) | (e != e_prev)))
    def _():
        wg_bf[...] = wg_ref[0].astype(BF16)
        wu_bf[...] = wu_ref[0].astype(BF16)
        wd_bf[...] = wd_ref[0].astype(BF16)

    @pl.when(used)
    def _():
        x = x_ref[...].astype(BF16)
        g = jnp.dot(x, wg_bf[...], preferred_element_type=F32)
        u = jnp.dot(x, wu_bf[...], preferred_element_type=F32)
        hid = (g / (1.0 + jnp.exp(-g)) * u).astype(BF16)
        y_ref[...] = jnp.dot(hid, wd_bf[...], preferred_element_type=F32)

    @pl.when(jnp.logical_not(used))
    def _():
        y_ref[...] = jnp.zeros(y_ref.shape, F32)


def _expert_call(xs, blk_expert, n_used, w_gate, w_up, w_down):
    n_rows, d_model = xs.shape
    d_exp = w_gate.shape[2]
    tb = MOE_TILE
    blk = lambda i, be, nu: jnp.minimum(i, nu[0] - 1)
    rows = pl.BlockSpec((tb, d_model), lambda i, be, nu: (blk(i, be, nu), 0))
    weight = lambda shape: pl.BlockSpec((1,) + shape, lambda i, be, nu: (be[blk(i, be, nu)], 0, 0))
    grid_spec = pltpu.PrefetchScalarGridSpec(
        num_scalar_prefetch=2,
        grid=(n_rows // tb,),
        in_specs=[rows, weight((d_model, d_exp)), weight((d_model, d_exp)),
                  weight((d_exp, d_model))],
        out_specs=pl.BlockSpec((tb, d_model), lambda i, be, nu: (i, 0)),
        scratch_shapes=[pltpu.VMEM((d_model, d_exp), BF16), pltpu.VMEM((d_model, d_exp), BF16),
                        pltpu.VMEM((d_exp, d_model), BF16)])
    return pl.pallas_call(
        _expert_kernel, grid_spec=grid_spec,
        out_shape=jax.ShapeDtypeStruct((n_rows, d_model), F32),
        compiler_params=_cparams(1), name="expert_mlp")(
            blk_expert, n_used, xs, w_gate, w_up, w_down)


def _combine_kernel(pstart_ref, idx_ref, h_ref, meta_ref, y_ref, o_ref, ybuf_ref, sem):
    tm = h_ref.shape[0]

    def row_copy(r, k):
        src = _assignment_row(pstart_ref, idx_ref, r, k)
        return pltpu.make_async_copy(y_ref.at[pl.ds(src, 1)], ybuf_ref.at[k, pl.ds(r, 1)], sem)

    _for_each_assignment(tm, lambda r, k: row_copy(r, k).start())
    _for_each_assignment(tm, lambda r, k: row_copy(r, k).wait())
    meta = meta_ref[...]
    out = h_ref[...]
    for k in range(TOP_K_INNER):
        out = out + meta[:, META_GATE + k:META_GATE + k + 1] * ybuf_ref[k]
    o_ref[...] = out


def _combine_call(h2, meta, y_rows, pstart, idx):
    n_tok, d_model = h2.shape
    tm = idx.shape[2]
    grid_spec = pltpu.PrefetchScalarGridSpec(
        num_scalar_prefetch=1,
        grid=(n_tok // tm,),
        in_specs=[pl.BlockSpec((1,) + idx.shape[1:], lambda i, ps: (i, 0, 0),
                               memory_space=pltpu.SMEM),
                  pl.BlockSpec((tm, d_model), lambda i, ps: (i, 0)),
                  pl.BlockSpec((tm, LANES), lambda i, ps: (i, 0)),
                  pl.BlockSpec(memory_space=pl.ANY)],
        out_specs=pl.BlockSpec((tm, d_model), lambda i, ps: (i, 0)),
        scratch_shapes=[pltpu.VMEM((TOP_K_INNER, tm, d_model), F32), pltpu.SemaphoreType.DMA(())])
    return pl.pallas_call(
        _combine_kernel, grid_spec=grid_spec,
        out_shape=jax.ShapeDtypeStruct((n_tok, d_model), F32),
        compiler_params=_cparams(1), name="moe_combine")(pstart, idx, h2, meta, y_rows)


def _moe(h2, f, meta, count_row, w_gate, w_up, w_down):
    pstart, pend, blk_expert, n_used, idx, n_rows = _dispatch_plan(meta, count_row, MOE_TILE, ROW_TILE)
    xs = _dispatch_call(f, pstart, pend, idx, n_rows)
    y_rows = _expert_call(xs, blk_expert, n_used, w_gate, w_up, w_down)
    return _combine_call(h2, meta, y_rows, pstart, idx)


def _proj_col_gain(diff_qk_norm, ca_qk_norm, d_model):
    sb_w, df_w = d_model // 4, d_model // 2
    ones = lambda n: jnp.ones((n,), F32)
    rep = lambda g, n: jnp.tile(g.astype(F32), n // HEAD_DIM)
    return jnp.concatenate([
        ones(3 * sb_w),
        rep(diff_qk_norm[0], df_w) * Q_FOLD, rep(diff_qk_norm[1], df_w), ones(df_w),
        rep(ca_qk_norm[0], sb_w) * Q_FOLD, rep(ca_qk_norm[1], sb_w), ones(sb_w)]).reshape(1, -1)


def kernel(x, t5_bias, attn_norm, w_in, diff_qk_norm, diff_lambda, diff_subln, ca_qk_norm,
           ca_rel_bias, w_out, ffn_norm, router_group_w, router_group_b, router_expert_w,
           router_expert_b, expert_w_gate, expert_w_up, expert_w_down):
    batch, seq, d_model = x.shape
    depth = w_in.shape[0]
    n_tok = batch * seq
    lane_group = jnp.arange(LANES) // HEAD_DIM
    group_mean = ((lane_group[:, None] == lane_group[None, :]).astype(F32) / HEAD_DIM).astype(BF16)

    h = x.reshape(n_tok, d_model)
    for l in range(depth):
        lam_init = 0.8 - 0.6 * math.exp(-0.3 * l)
        col_gain = _proj_col_gain(diff_qk_norm[l], ca_qk_norm[l], d_model)
        proj = _proj_call(h, attn_norm[l], w_in[l].astype(BF16), col_gain, group_mean)
        ya, yd, yc = _attention_calls(proj, batch, t5_bias, diff_lambda[l], diff_subln[l],
                                      ca_rel_bias[l], lam_init)
        r_w = jnp.concatenate([router_group_w[l], router_expert_w[l]], axis=1).astype(F32)
        r_w = jnp.pad(r_w, ((0, 0), (0, LANES - r_w.shape[1])))
        r_hi, r_lo = _split_bf16(r_w)
        r_b = jnp.concatenate([router_group_b[l], router_expert_b[l]]).astype(F32)
        r_b = jnp.pad(r_b, (0, LANES - r_b.shape[0])).reshape(1, LANES)
        h2, f, meta, count_row = _out_proj_call(h, ya, yd, yc, w_out[l].astype(BF16), ffn_norm[l],
                                                r_hi, r_lo, r_b)
        h = _moe(h2, f, meta, count_row, expert_w_gate[l], expert_w_up[l], expert_w_down[l])
    return h.reshape(batch, seq, d_model)
```

```python
import functools
import math

import jax
import jax.numpy as jnp
from jax import lax
from jax.experimental import pallas as pl
from jax.experimental.pallas import tpu as pltpu

F32 = jnp.float32
BF16 = jnp.bfloat16

LANES = 128
HEAD_DIM = 64
CHUNK = 64
CA_LEFT_CHUNKS = 8
CA_REL_CLIP = 128
T5_BUCKETS = 32
T5_MAX_DIST = 256
N_GROUPS = 4
EXPERTS_PER_GROUP = 8
N_EXPERTS = N_GROUPS * EXPERTS_PER_GROUP
TOP_K_INNER = 2
EPS = 1e-6
NEG = -1e30
LOG2E = math.log2(math.e)
Q_FOLD = HEAD_DIM ** -0.5 * LOG2E

ROW_TILE = 512
ATT_TILE = 256
MOE_TILE = 256
VMEM_LIMIT = 48 * 1024 * 1024


def _cparams(n_axes):
    return pltpu.CompilerParams(dimension_semantics=("arbitrary",) * n_axes,
                                vmem_limit_bytes=VMEM_LIMIT)


def _split_bf16(x):
    hi = x.astype(BF16)
    lo = (x - hi.astype(F32)).astype(BF16)
    return hi, lo


def _col_kind(col, d_model):
    sb_w, df_w = d_model // 4, d_model // 2
    sb_end = 3 * sb_w
    df_end = sb_end + 3 * df_w
    if col < sb_w:
        return "scale"
    if col < sb_end:
        return "plain"
    if col < sb_end + 2 * df_w:
        return "norm"
    if col < df_end:
        return "plain"
    if col < df_end + 2 * sb_w:
        return "norm"
    return "plain"


def _proj_kernel(h_ref, g_ref, w_ref, cg_ref, gm_ref, o_ref, *, d_model, col_chunk):
    x = h_ref[...]
    ms = jnp.mean(x * x, axis=-1, keepdims=True)
    xn = (x * lax.rsqrt(ms + EPS) * g_ref[...]).astype(BF16)
    n_out = w_ref.shape[1]
    gm = gm_ref[...]
    for c0 in range(0, n_out, col_chunk):
        acc = jnp.dot(xn, w_ref[:, c0:c0 + col_chunk], preferred_element_type=F32)
        for s0 in range(0, col_chunk, LANES):
            col = c0 + s0
            blk = acc[:, s0:s0 + LANES]
            kind = _col_kind(col, d_model)
            if kind == "norm":
                hi, lo = _split_bf16(blk * blk)
                msq = (jnp.dot(hi, gm, preferred_element_type=F32)
                       + jnp.dot(lo, gm, preferred_element_type=F32))
                blk = blk * lax.rsqrt(msq + EPS) * cg_ref[:, col:col + LANES]
            elif kind == "scale":
                blk = blk * Q_FOLD
            o_ref[:, col:col + LANES] = blk.astype(BF16)


def _proj_call(h, gain, w_bf, col_gain, group_mean):
    n_tok, d_model = h.shape
    n_out = w_bf.shape[1]
    tm = ROW_TILE
    return pl.pallas_call(
        functools.partial(_proj_kernel, d_model=d_model, col_chunk=512),
        grid=(n_tok // tm,),
        in_specs=[pl.BlockSpec((tm, d_model), lambda i: (i, 0)),
                  pl.BlockSpec((1, d_model), lambda i: (0, 0)),
                  pl.BlockSpec((d_model, n_out), lambda i: (0, 0)),
                  pl.BlockSpec((1, n_out), lambda i: (0, 0)),
                  pl.BlockSpec((LANES, LANES), lambda i: (0, 0))],
        out_specs=pl.BlockSpec((tm, n_out), lambda i: (i, 0)),
        out_shape=jax.ShapeDtypeStruct((n_tok, n_out), BF16),
        compiler_params=_cparams(1), name="norm_in_proj")(
            h, gain.reshape(1, d_model), w_bf, col_gain, group_mean)


def _stack_halves(q):
    lane = lax.broadcasted_iota(jnp.int32, q.shape, 1)
    zero = jnp.zeros_like(q)
    return jnp.concatenate([jnp.where(lane < HEAD_DIM, q, zero),
                            jnp.where(lane >= HEAD_DIM, q, zero)], axis=0)


def _scores(q2, k):
    return lax.dot_general(q2, k, (((1,), (1,)), ((), ())), preferred_element_type=F32)


def _lane_tile(x, width):
    reps = width // LANES
    return x if reps == 1 else jnp.concatenate([x] * reps, axis=1)


def _lane_fold(x, op):
    out = x[:, :LANES]
    for c in range(LANES, x.shape[1], LANES):
        out = op(out, x[:, c:c + LANES])
    return out


def _rows_tile(ref, j, tk):
    return ref[0, pl.ds(pl.multiple_of(j * tk, tk), tk), :]


def _for_each(n, fn, ways):
    def group(g, c):
        for w in range(ways):
            fn(g * ways + w)
        return c

    lax.fori_loop(0, n // ways, group, 0)
    part = ways // 2
    while part >= 1:
        start = n // (2 * part) * (2 * part)

        def tail(start=start, part=part):
            for w in range(part):
                fn(start + w)

        pl.when(n % (2 * part) >= part)(tail)
        part //= 2


def _merge_halves(y, tq):
    lane = lax.broadcasted_iota(jnp.int32, (tq, LANES), 1)
    return jnp.where(lane < HEAD_DIM, y[:tq], y[tq:])


SOFTPLUS_CLAMP = 126.0
F32_EXP2_UNDERFLOW = 152.0


def _sb_kernel(q_ref, k_ref, v_ref, u_ref, o_ref, acc_ref, skipped_ref, *, tq):
    i = pl.program_id(2)
    q2 = _stack_halves(q_ref[0])
    rows = 2 * tq
    acc_ref[...] = jnp.zeros(acc_ref.shape, F32)
    skipped_ref[...] = jnp.zeros(skipped_ref.shape, F32)

    def tile(j, diagonal):
        z = _scores(q2, _rows_tile(k_ref, j, tq))
        sp = jnp.maximum(jnp.log(1.0 + jnp.exp2(jnp.minimum(z, SOFTPLUS_CLAMP))) * LOG2E, z)
        if diagonal:
            row = lax.broadcasted_iota(jnp.int32, (rows, tq), 0)
            col = lax.broadcasted_iota(jnp.int32, (rows, tq), 1)
            earlier = col < jnp.where(row >= tq, row - tq, row)
            sp = jnp.where(earlier, sp, 0.0)
        hi, lo = _split_bf16(sp)
        suffix = jnp.dot(jnp.concatenate([hi, lo], axis=1), u_ref[...],
                         preferred_element_type=F32)
        skipped = skipped_ref[...]
        w = jnp.exp2(z - suffix - _lane_tile(skipped, tq))
        if diagonal:
            w = jnp.where(earlier, w, 0.0)
        acc_ref[...] += jnp.dot(w.astype(BF16), _rows_tile(v_ref, j, tq),
                                preferred_element_type=F32)
        skipped = skipped + jnp.broadcast_to(suffix[:, :1], skipped.shape)
        skipped_ref[...] = skipped
        return jnp.min(skipped)

    def more(state):
        t, least_skipped = state
        return (t <= i) & (least_skipped < F32_EXP2_UNDERFLOW)

    def step(state):
        t, _ = state
        return t + 1, tile(i - t, False)

    lax.while_loop(more, step, (jnp.int32(1), tile(i, True)))
    o_ref[0] = _merge_halves(acc_ref[...], tq).astype(o_ref.dtype)


def _diff_kernel(far_ref, q_ref, k_ref, v_ref, near_ref, lam_ref, subln_ref, o_ref,
                 s_ref, m_ref, l_ref, acc_ref, *, tq, lam_init):
    h = pl.program_id(1)
    i = pl.program_id(2)
    q2 = _stack_halves(q_ref[0])
    rows = 2 * tq
    m_ref[...] = jnp.full(m_ref.shape, NEG, F32)

    def score(j, bias):
        s = _scores(q2, _rows_tile(k_ref, j, tq))
        if jnp.ndim(bias) == 2:
            s = (s.reshape(2, tq, tq) + bias[None]).reshape(rows, tq)
        else:
            s = s + bias
        s_ref[j] = s
        m_ref[...] = jnp.maximum(m_ref[...], _lane_fold(s, jnp.maximum))

    score(i, near_ref[0, 0])
    pl.when(i >= 1)(lambda: score(i - 1, near_ref[0, 1]))
    far = far_ref[h]
    _for_each(jnp.maximum(i - 1, 0), lambda j: score(j, far), 4)

    m = jnp.max(m_ref[...], axis=1, keepdims=True)
    m_ref[...] = jnp.broadcast_to(m, m_ref.shape)
    l_ref[...] = jnp.zeros(l_ref.shape, F32)
    acc_ref[...] = jnp.zeros(acc_ref.shape, F32)

    def weigh(j):
        p = jnp.exp2(s_ref[j] - _lane_tile(m_ref[...], tq))
        l_ref[...] += _lane_fold(p, jnp.add)
        acc_ref[...] += jnp.dot(p.astype(BF16), _rows_tile(v_ref, j, tq),
                                preferred_element_type=F32)

    _for_each(i + 1, weigh, 4)

    y = acc_ref[...] / jnp.sum(l_ref[...], axis=1, keepdims=True)
    lmb = lam_ref[...]
    lam = (jnp.exp(jnp.sum(lmb[0:1] * lmb[1:2], axis=-1, keepdims=True))
           - jnp.exp(jnp.sum(lmb[2:3] * lmb[3:4], axis=-1, keepdims=True)) + lam_init)
    y = y[:tq] - lam * y[tq:]
    ms = jnp.mean(y * y, axis=-1, keepdims=True)
    y = y * lax.rsqrt(ms + EPS) * subln_ref[...] * (1.0 - lam_init)
    o_ref[0] = y.astype(o_ref.dtype)


def _ca_kernel(q_ref, k_ref, v_ref, bias_ref, o_ref, *, tq, n_near):
    i = pl.program_id(2)
    q2 = _stack_halves(q_ref[0])
    tiles = []
    for d in range(n_near):
        j = jnp.maximum(i - d, 0)
        s = _scores(q2, _rows_tile(k_ref, j, tq)) + bias_ref[0, d]
        if d > 0:
            s = s + jnp.where(i >= d, 0.0, NEG)
        tiles.append((j, s))
    m = _lane_fold(tiles[0][1], jnp.maximum)
    for _, s in tiles[1:]:
        m = jnp.maximum(m, _lane_fold(s, jnp.maximum))
    m = _lane_tile(jnp.broadcast_to(jnp.max(m, axis=1, keepdims=True), m.shape), tq)
    l = None
    acc = None
    for j, s in tiles:
        p = jnp.exp2(s - m)
        pv = jnp.dot(p.astype(BF16), _rows_tile(v_ref, j, tq), preferred_element_type=F32)
        psum = _lane_fold(p, jnp.add)
        l = psum if l is None else l + psum
        acc = pv if acc is None else acc + pv
    y = acc / jnp.sum(l, axis=1, keepdims=True)
    o_ref[0] = _merge_halves(y, tq).astype(o_ref.dtype)


def _t5_bucket(rel):
    nb = T5_BUCKETS // 2
    max_exact = nb // 2
    ret = jnp.where(rel > 0, nb, 0)
    n = jnp.abs(rel)
    large = max_exact + (jnp.log(jnp.maximum(n, 1).astype(F32) / max_exact)
                         / math.log(T5_MAX_DIST / max_exact) * (nb - max_exact)).astype(jnp.int32)
    large = jnp.minimum(large, nb - 1)
    return ret + jnp.where(n < max_exact, n, large)


def _toeplitz_tile(bias_of_rel, tq, d):
    span = 2 * tq
    x = jnp.arange(span, dtype=jnp.int32)
    x = jnp.where(x < tq, x, x - span)
    vec = jnp.transpose(bias_of_rel(x - d * tq)).astype(F32) * LOG2E
    flat = jnp.tile(vec, (1, tq))[:, :tq * (span - 1)]
    return flat.reshape(vec.shape[0], tq, span - 1)[:, :, :tq]


def _diff_bias_tables(t5_bias, tq):
    assert tq >= T5_MAX_DIST and tq % CHUNK == 0
    r = jnp.arange(tq)[:, None]
    c = jnp.arange(tq)[None, :]
    bias_of_rel = lambda rel: t5_bias[_t5_bucket(rel)]
    diag = jnp.where((c // CHUNK) <= (r // CHUNK), _toeplitz_tile(bias_of_rel, tq, 0), NEG)
    far = t5_bias[_t5_bucket(jnp.int32(-2 * tq))].astype(F32) * LOG2E
    return jnp.stack([diag, _toeplitz_tile(bias_of_rel, tq, 1)], axis=1), far


def _ca_bias_tables(rel_bias, tq):
    assert (CA_LEFT_CHUNKS * CHUNK) % tq == 0 and tq % CHUNK == 0
    n_near = CA_LEFT_CHUNKS * CHUNK // tq + 1
    r = jnp.arange(tq)[:, None]
    c = jnp.arange(tq)[None, :]
    bias_of_rel = lambda rel: rel_bias[jnp.clip(rel, -CA_REL_CLIP, CA_REL_CLIP) + CA_REL_CLIP]
    tiles = []
    for d in range(n_near):
        gap = d * (tq // CHUNK) + r // CHUNK - c // CHUNK
        tiles.append(jnp.where((gap >= 0) & (gap <= CA_LEFT_CHUNKS),
                               _toeplitz_tile(bias_of_rel, tq, d), NEG))
    t = jnp.stack(tiles, axis=1)
    n_heads = t.shape[0]
    t = t.reshape(n_heads // 2, 2, n_near, tq, tq).transpose(0, 2, 1, 3, 4)
    return t.reshape(n_heads // 2, n_near, 2 * tq, tq), n_near


def _attention_calls(proj, batch, t5_bias, diff_lambda, diff_subln, ca_rel_bias, lam_init):
    n_tok, n_proj = proj.shape
    seq = n_tok // batch
    d_model = n_proj // 3
    sb_w = d_model // 4
    proj3 = proj.reshape(batch, seq, n_proj)
    tq = ATT_TILE
    nq = seq // tq
    sb_blocks = sb_w // LANES
    df_heads = (d_model // 2) // LANES
    sb_q0 = 0
    df_q0 = 3 * sb_blocks
    ca_q0 = df_q0 + 3 * df_heads

    def qkv_specs(q0, stride):
        return [pl.BlockSpec((1, tq, LANES), lambda b, h, i: (b, i, q0 + h)),
                pl.BlockSpec((1, seq, LANES), lambda b, h, i: (b, 0, q0 + stride + h)),
                pl.BlockSpec((1, seq, LANES), lambda b, h, i: (b, 0, q0 + 2 * stride + h))]

    out_spec = pl.BlockSpec((1, tq, LANES), lambda b, h, i: (b, i, h))
    stacked = pltpu.VMEM((2 * tq, LANES), F32)

    lower = (jnp.arange(tq)[:, None] >= jnp.arange(tq)[None, :]).astype(BF16)
    u = jnp.concatenate([lower, lower], axis=0)
    ya = pl.pallas_call(
        functools.partial(_sb_kernel, tq=tq),
        grid=(batch, sb_blocks, nq),
        in_specs=qkv_specs(sb_q0, sb_blocks) + [pl.BlockSpec(u.shape, lambda b, h, i: (0, 0))],
        out_specs=out_spec,
        out_shape=jax.ShapeDtypeStruct((batch, seq, sb_w), BF16),
        scratch_shapes=[stacked, stacked],
        compiler_params=_cparams(3), name="stick_breaking_attn")(proj3, proj3, proj3, u)

    near, far = _diff_bias_tables(t5_bias, tq)
    yd = pl.pallas_call(
        functools.partial(_diff_kernel, tq=tq, lam_init=lam_init),
        grid=(batch, df_heads, nq),
        in_specs=[pl.BlockSpec(memory_space=pltpu.SMEM)] + qkv_specs(df_q0, df_heads) + [
            pl.BlockSpec((1, 2, tq, tq), lambda b, h, i: (h, 0, 0, 0)),
            pl.BlockSpec((4, HEAD_DIM), lambda b, h, i: (0, 0)),
            pl.BlockSpec((1, LANES), lambda b, h, i: (0, 0))],
        out_specs=out_spec,
        out_shape=jax.ShapeDtypeStruct((batch, seq, d_model // 2), BF16),
        scratch_shapes=[pltpu.VMEM((nq, 2 * tq, tq), F32), stacked, stacked, stacked],
        compiler_params=_cparams(3), name="differential_attn")(
            far, proj3, proj3, proj3, near, diff_lambda.astype(F32),
            diff_subln.astype(F32).reshape(1, LANES))

    ca_bias, n_near = _ca_bias_tables(ca_rel_bias, tq)
    yc = pl.pallas_call(
        functools.partial(_ca_kernel, tq=tq, n_near=n_near),
        grid=(batch, sb_blocks, nq),
        in_specs=qkv_specs(ca_q0, sb_blocks) + [
            pl.BlockSpec((1, n_near, 2 * tq, tq), lambda b, h, i: (h, 0, 0, 0))],
        out_specs=out_spec,
        out_shape=jax.ShapeDtypeStruct((batch, seq, sb_w), BF16),
        compiler_params=_cparams(3), name="chunked_rel_attn")(proj3, proj3, proj3, ca_bias)

    return (ya.reshape(n_tok, sb_w), yd.reshape(n_tok, d_model // 2), yc.reshape(n_tok, sb_w))


META_E, META_GATE, META_RANK = 0, 2, 4
META_ROWS = 8


def _row_min_lane(mask, lane_f):
    return jnp.min(jnp.where(mask, lane_f, float(LANES)), axis=1, keepdims=True)


def _out_proj_kernel(h_ref, ya_ref, yd_ref, yc_ref, wo_ref, g_ref, rhi_ref, rlo_ref, rb_ref,
                     ltri_ref, h2_ref, f_ref, meta_ref, meta_t_ref, count_ref, run_ref):
    wa = ya_ref.shape[1]
    wd = yd_ref.shape[1]
    h2 = (h_ref[...]
          + jnp.dot(ya_ref[...], wo_ref[0:wa, :], preferred_element_type=F32)
          + jnp.dot(yd_ref[...], wo_ref[wa:wa + wd, :], preferred_element_type=F32)
          + jnp.dot(yc_ref[...], wo_ref[wa + wd:, :], preferred_element_type=F32))
    h2_ref[...] = h2
    ms = jnp.mean(h2 * h2, axis=-1, keepdims=True)
    f = h2 * lax.rsqrt(ms + EPS) * g_ref[...]
    f_ref[...] = f
    f_hi, f_lo = _split_bf16(f)
    logits = (jnp.dot(f_hi, rhi_ref[...], preferred_element_type=F32)
              + jnp.dot(f_lo, rhi_ref[...], preferred_element_type=F32)
              + jnp.dot(f_hi, rlo_ref[...], preferred_element_type=F32)
              + rb_ref[...])

    lane = lax.broadcasted_iota(jnp.int32, logits.shape, 1)
    lane_f = lane.astype(F32)
    is_group = lane < N_GROUPS
    gl = jnp.where(is_group, logits, NEG)
    g_max = jnp.max(gl, axis=1, keepdims=True)
    g_sel = _row_min_lane(gl == g_max, lane_f)
    g_w = 1.0 / jnp.sum(jnp.where(is_group, jnp.exp(logits - g_max), 0.0), axis=1, keepdims=True)
    first_lane = float(N_GROUPS) + float(EXPERTS_PER_GROUP) * g_sel
    in_group = (lane_f >= first_lane) & (lane_f < first_lane + float(EXPERTS_PER_GROUP))
    el = jnp.where(in_group, logits, NEG)
    m1 = jnp.max(el, axis=1, keepdims=True)
    i1 = _row_min_lane(el == m1, lane_f)
    el = jnp.where(lane_f == i1, NEG, el)
    m2 = jnp.max(el, axis=1, keepdims=True)
    i2 = _row_min_lane(el == m2, lane_f)
    r = jnp.exp(m2 - m1)
    p1 = 1.0 / (1.0 + r)
    gate1 = g_w * p1
    gate2 = g_w * (r * p1)

    @pl.when(pl.program_id(0) == 0)
    def _():
        run_ref[...] = jnp.zeros(run_ref.shape, F32)

    hit1 = lane_f == i1
    hit2 = lane_f == i2
    one_hot = jnp.where(hit1 | hit2, 1.0, 0.0)
    pos = run_ref[...] + jnp.dot(ltri_ref[...], one_hot.astype(BF16), preferred_element_type=F32)
    rank1 = jnp.sum(jnp.where(hit1, pos, 0.0), axis=1, keepdims=True)
    rank2 = jnp.sum(jnp.where(hit2, pos, 0.0), axis=1, keepdims=True)
    run_ref[...] += jnp.sum(one_hot, axis=0, keepdims=True)
    count_ref[...] = run_ref[...]

    meta = jnp.zeros(logits.shape, F32)
    for at, val in ((META_E, i1 - N_GROUPS), (META_E + 1, i2 - N_GROUPS), (META_GATE, gate1),
                    (META_GATE + 1, gate2), (META_RANK, rank1), (META_RANK + 1, rank2)):
        meta = jnp.where(lane == at, val, meta)
    meta_ref[...] = meta
    meta_t_ref[...] = jnp.transpose(meta)[:META_ROWS]


def _out_proj_call(h, ya, yd, yc, wo_bf, gain, r_hi, r_lo, r_bias):
    n_tok, d_model = h.shape
    tm = ROW_TILE
    row = lambda w: pl.BlockSpec((tm, w), lambda i: (i, 0))
    full = lambda a: pl.BlockSpec(a.shape, lambda i: (0, 0))
    gain = gain.reshape(1, d_model)
    ltri = (jnp.arange(tm)[:, None] > jnp.arange(tm)[None, :]).astype(BF16)
    return pl.pallas_call(
        _out_proj_kernel,
        grid=(n_tok // tm,),
        in_specs=[row(d_model), row(ya.shape[1]), row(yd.shape[1]), row(yc.shape[1]),
                  full(wo_bf), full(gain), full(r_hi), full(r_lo), full(r_bias), full(ltri)],
        out_specs=[row(d_model), row(d_model), row(LANES),
                   pl.BlockSpec((META_ROWS, tm), lambda i: (0, i)),
                   pl.BlockSpec((1, LANES), lambda i: (0, 0))],
        out_shape=[jax.ShapeDtypeStruct((n_tok, d_model), F32),
                   jax.ShapeDtypeStruct((n_tok, d_model), F32),
                   jax.ShapeDtypeStruct((n_tok, LANES), F32),
                   jax.ShapeDtypeStruct((META_ROWS, n_tok), F32),
                   jax.ShapeDtypeStruct((1, LANES), F32)],
        scratch_shapes=[pltpu.VMEM((1, LANES), F32)],
        compiler_params=_cparams(1), name="out_proj_router")(
            h, ya, yd, yc, wo_bf, gain, r_hi, r_lo, r_bias, ltri)


def _dispatch_plan(meta_t, count_row, tb, tm):
    n_tok = meta_t.shape[1]
    counts = count_row[0, N_GROUPS:N_GROUPS + N_EXPERTS].astype(jnp.int32)
    padded = (counts + tb - 1) // tb * tb
    e = jnp.arange(N_EXPERTS)
    pend = jnp.sum(jnp.where(e[:, None] >= e[None, :], padded[None, :], 0), axis=1)
    pstart = pend - padded
    n_rows = n_tok * TOP_K_INNER + N_EXPERTS * tb
    block_start = jnp.arange(n_rows // tb, dtype=jnp.int32) * tb
    blk_expert = jnp.minimum(jnp.sum((pend[None, :] <= block_start[:, None]).astype(jnp.int32), axis=1),
                             N_EXPERTS - 1)
    n_used = pend[-1:] // tb
    expert = meta_t[META_E:META_E + TOP_K_INNER].astype(jnp.int32)
    rank = meta_t[META_RANK:META_RANK + TOP_K_INNER].astype(jnp.int32)
    seg_start = jnp.sum(jnp.where(expert[..., None] == e, pstart, 0), axis=-1)
    idx = (seg_start + rank).reshape(TOP_K_INNER, n_tok // tm, tm).transpose(1, 0, 2)
    return pend.astype(jnp.int32), blk_expert, n_used, idx, n_rows


def _for_each_assignment(tm, fn):
    def body(r, c):
        for k in range(TOP_K_INNER):
            fn(r, k)
        return c

    lax.fori_loop(0, tm, body, 0, unroll=8)


def _dispatch_kernel(pend_ref, idx_ref, f_ref, xs_ref, zero_ref, row_sem, zero_sem, *, tb):
    tm = f_ref.shape[0]

    @pl.when(pl.program_id(0) == 0)
    def _():
        zero_ref[...] = jnp.zeros(zero_ref.shape, F32)

        def block_copy(start):
            return pltpu.make_async_copy(zero_ref, xs_ref.at[pl.ds(pl.multiple_of(start, tb), tb)],
                                         zero_sem)

        for e in range(N_EXPERTS):
            block_copy(jnp.maximum(pend_ref[e] - tb, 0)).start()
        for e in range(N_EXPERTS):
            block_copy(jnp.maximum(pend_ref[e] - tb, 0)).wait()

        def fill(b, c):
            block_copy(b * tb).start()
            block_copy(b * tb).wait()
            return c

        first_unused = lax.shift_right_logical(pend_ref[N_EXPERTS - 1], tb.bit_length() - 1)
        lax.fori_loop(first_unused, xs_ref.shape[0] // tb, fill, 0)

    def row_copy(r, k):
        return pltpu.make_async_copy(f_ref.at[pl.ds(r, 1)], xs_ref.at[pl.ds(idx_ref[0, k, r], 1)],
                                     row_sem)

    _for_each_assignment(tm, lambda r, k: row_copy(r, k).start())
    _for_each_assignment(tm, lambda r, k: row_copy(r, k).wait())


def _dispatch_call(f, pend, idx, n_rows):
    n_tok, d_model = f.shape
    tm = idx.shape[2]
    tb = MOE_TILE
    assert tb & (tb - 1) == 0
    grid_spec = pltpu.PrefetchScalarGridSpec(
        num_scalar_prefetch=1,
        grid=(n_tok // tm,),
        in_specs=[pl.BlockSpec((1,) + idx.shape[1:], lambda i, pe: (i, 0, 0),
                               memory_space=pltpu.SMEM),
                  pl.BlockSpec((tm, d_model), lambda i, pe: (i, 0))],
        out_specs=pl.BlockSpec(memory_space=pl.ANY),
        scratch_shapes=[pltpu.VMEM((tb, d_model), F32), pltpu.SemaphoreType.DMA(()),
                        pltpu.SemaphoreType.DMA(())])
    return pl.pallas_call(
        functools.partial(_dispatch_kernel, tb=tb), grid_spec=grid_spec,
        out_shape=jax.ShapeDtypeStruct((n_rows, d_model), F32),
        compiler_params=_cparams(1), name="moe_dispatch")(pend, idx, f)


def _expert_kernel(blk_expert_ref, n_used_ref, x_ref, wg_ref, wu_ref, wd_ref, y_ref,
                   wg_bf, wu_bf, wd_bf):
    i = pl.program_id(0)
    used = i < n_used_ref[0]
    e = blk_expert_ref[i]
    e_prev = blk_expert_ref[jnp.maximum(i - 1, 0)]

    @pl.when(used & ((i == 0) | (e != e_prev)))
    def _():
        wg_bf[...] = wg_ref[0].astype(BF16)
        wu_bf[...] = wu_ref[0].astype(BF16)
        wd_bf[...] = wd_ref[0].astype(BF16)

    @pl.when(used)
    def _():
        x = x_ref[...].astype(BF16)
        g = jnp.dot(x, wg_bf[...], preferred_element_type=F32)
        u = jnp.dot(x, wu_bf[...], preferred_element_type=F32)
        hid = (g / (1.0 + jnp.exp(-g)) * u).astype(BF16)
        y_ref[...] = jnp.dot(hid, wd_bf[...], preferred_element_type=F32)

    @pl.when(jnp.logical_not(used))
    def _():
        y_ref[...] = jnp.zeros(y_ref.shape, F32)


def _expert_call(xs, blk_expert, n_used, w_gate, w_up, w_down):
    n_rows, d_model = xs.shape
    d_exp = w_gate.shape[2]
    tb = MOE_TILE
    blk = lambda i, be, nu: jnp.minimum(i, nu[0] - 1)
    rows = pl.BlockSpec((tb, d_model), lambda i, be, nu: (blk(i, be, nu), 0))
    weight = lambda shape: pl.BlockSpec((1,) + shape, lambda i, be, nu: (be[blk(i, be, nu)], 0, 0))
    grid_spec = pltpu.PrefetchScalarGridSpec(
        num_scalar_prefetch=2,
        grid=(n_rows // tb,),
        in_specs=[rows, weight((d_model, d_exp)), weight((d_model, d_exp)),
                  weight((d_exp, d_model))],
        out_specs=pl.BlockSpec((tb, d_model), lambda i, be, nu: (i, 0)),
        scratch_shapes=[pltpu.VMEM((d_model, d_exp), BF16), pltpu.VMEM((d_model, d_exp), BF16),
                        pltpu.VMEM((d_exp, d_model), BF16)])
    return pl.pallas_call(
        _expert_kernel, grid_spec=grid_spec,
        out_shape=jax.ShapeDtypeStruct((n_rows, d_model), F32),
        compiler_params=_cparams(1), name="expert_mlp")(
            blk_expert, n_used, xs, w_gate, w_up, w_down)


def _combine_kernel(idx_ref, h_ref, meta_ref, y_ref, o_ref, ybuf_ref, sem):
    tm = h_ref.shape[0]

    def row_copy(r, k):
        return pltpu.make_async_copy(y_ref.at[pl.ds(idx_ref[0, k, r], 1)],
                                     ybuf_ref.at[k, pl.ds(r, 1)], sem)

    _for_each_assignment(tm, lambda r, k: row_copy(r, k).start())
    _for_each_assignment(tm, lambda r, k: row_copy(r, k).wait())
    meta = meta_ref[...]
    out = h_ref[...]
    for k in range(TOP_K_INNER):
        out = out + meta[:, META_GATE + k:META_GATE + k + 1] * ybuf_ref[k]
    o_ref[...] = out


def _combine_call(h2, meta, y_rows, idx):
    n_tok, d_model = h2.shape
    tm = idx.shape[2]
    return pl.pallas_call(
        _combine_kernel,
        grid=(n_tok // tm,),
        in_specs=[pl.BlockSpec((1,) + idx.shape[1:], lambda i: (i, 0, 0), memory_space=pltpu.SMEM),
                  pl.BlockSpec((tm, d_model), lambda i: (i, 0)),
                  pl.BlockSpec((tm, LANES), lambda i: (i, 0)),
                  pl.BlockSpec(memory_space=pl.ANY)],
        out_specs=pl.BlockSpec((tm, d_model), lambda i: (i, 0)),
        out_shape=jax.ShapeDtypeStruct((n_tok, d_model), F32),
        scratch_shapes=[pltpu.VMEM((TOP_K_INNER, tm, d_model), F32), pltpu.SemaphoreType.DMA(())],
        compiler_params=_cparams(1), name="moe_combine")(idx, h2, meta, y_rows)


def _moe(h2, f, meta, meta_t, count_row, w_gate, w_up, w_down):
    pend, blk_expert, n_used, idx, n_rows = _dispatch_plan(meta_t, count_row, MOE_TILE, ROW_TILE)
    xs = _dispatch_call(f, pend, idx, n_rows)
    y_rows = _expert_call(xs, blk_expert, n_used, w_gate, w_up, w_down)
    return _combine_call(h2, meta, y_rows, idx)


def _proj_col_gain(diff_qk_norm, ca_qk_norm, d_model):
    sb_w, df_w = d_model // 4, d_model // 2
    ones = lambda n: jnp.ones((n,), F32)
    rep = lambda g, n: jnp.tile(g.astype(F32), n // HEAD_DIM)
    return jnp.concatenate([
        ones(3 * sb_w),
        rep(diff_qk_norm[0], df_w) * Q_FOLD, rep(diff_qk_norm[1], df_w), ones(df_w),
        rep(ca_qk_norm[0], sb_w) * Q_FOLD, rep(ca_qk_norm[1], sb_w), ones(sb_w)]).reshape(1, -1)


def kernel(x, t5_bias, attn_norm, w_in, diff_qk_norm, diff_lambda, diff_subln, ca_qk_norm,
           ca_rel_bias, w_out, ffn_norm, router_group_w, router_group_b, router_expert_w,
           router_expert_b, expert_w_gate, expert_w_up, expert_w_down):
    batch, seq, d_model = x.shape
    depth = w_in.shape[0]
    n_tok = batch * seq
    lane_group = jnp.arange(LANES) // HEAD_DIM
    group_mean = ((lane_group[:, None] == lane_group[None, :]).astype(F32) / HEAD_DIM).astype(BF16)

    h = x.reshape(n_tok, d_model)
    for l in range(depth):
        lam_init = 0.8 - 0.6 * math.exp(-0.3 * l)
        col_gain = _proj_col_gain(diff_qk_norm[l], ca_qk_norm[l], d_model)
        proj = _proj_call(h, attn_norm[l], w_in[l].astype(BF16), col_gain, group_mean)
        ya, yd, yc = _attention_calls(proj, batch, t5_bias, diff_lambda[l], diff_subln[l],
                                      ca_rel_bias[l], lam_init)
        r_w = jnp.concatenate([router_group_w[l], router_expert_w[l]], axis=1).astype(F32)
        r_w = jnp.pad(r_w, ((0, 0), (0, LANES - r_w.shape[1])))
        r_hi, r_lo = _split_bf16(r_w)
        r_b = jnp.concatenate([router_group_b[l], router_expert_b[l]]).astype(F32)
        r_b = jnp.pad(r_b, (0, LANES - r_b.shape[0])).reshape(1, LANES)
        h2, f, meta, meta_t, count_row = _out_proj_call(h, ya, yd, yc, w_out[l].astype(BF16),
                                                        ffn_norm[l], r_hi, r_lo, r_b)
        h = _moe(h2, f, meta, meta_t, count_row, expert_w_gate[l], expert_w_up[l],
                 expert_w_down[l])
    return h.reshape(batch, seq, d_model)
```

```python
import functools
import math

import jax
import jax.numpy as jnp
from jax import lax
from jax.experimental import pallas as pl
from jax.experimental.pallas import tpu as pltpu

F32 = jnp.float32
BF16 = jnp.bfloat16

LANES = 128
HEAD_DIM = 64
CHUNK = 64
CA_LEFT_CHUNKS = 8
CA_REL_CLIP = 128
T5_BUCKETS = 32
T5_MAX_DIST = 256
N_GROUPS = 4
EXPERTS_PER_GROUP = 8
N_EXPERTS = N_GROUPS * EXPERTS_PER_GROUP
TOP_K_INNER = 2
EPS = 1e-6
NEG = -1e30
LOG2E = math.log2(math.e)
Q_FOLD = HEAD_DIM ** -0.5 * LOG2E

ROW_TILE = 512
ATT_TILE = 256
MOE_TILE = 256
VMEM_LIMIT = 48 * 1024 * 1024


def _cparams(n_axes):
    return pltpu.CompilerParams(dimension_semantics=("arbitrary",) * n_axes,
                                vmem_limit_bytes=VMEM_LIMIT)


def _split_bf16(x):
    hi = x.astype(BF16)
    lo = (x - hi.astype(F32)).astype(BF16)
    return hi, lo


def _col_kind(col, d_model):
    sb_w, df_w = d_model // 4, d_model // 2
    sb_end = 3 * sb_w
    df_end = sb_end + 3 * df_w
    if col < sb_w:
        return "scale"
    if col < sb_end:
        return "plain"
    if col < sb_end + 2 * df_w:
        return "norm"
    if col < df_end:
        return "plain"
    if col < df_end + 2 * sb_w:
        return "norm"
    return "plain"


def _proj_kernel(h_ref, g_ref, w_ref, cg_ref, gm_ref, o_ref, *, d_model, col_chunk):
    x = h_ref[...]
    ms = jnp.mean(x * x, axis=-1, keepdims=True)
    xn = (x * lax.rsqrt(ms + EPS) * g_ref[...]).astype(BF16)
    n_out = w_ref.shape[1]
    gm = gm_ref[...]
    for c0 in range(0, n_out, col_chunk):
        acc = jnp.dot(xn, w_ref[:, c0:c0 + col_chunk], preferred_element_type=F32)
        for s0 in range(0, col_chunk, LANES):
            col = c0 + s0
            blk = acc[:, s0:s0 + LANES]
            kind = _col_kind(col, d_model)
            if kind == "norm":
                hi, lo = _split_bf16(blk * blk)
                msq = (jnp.dot(hi, gm, preferred_element_type=F32)
                       + jnp.dot(lo, gm, preferred_element_type=F32))
                blk = blk * lax.rsqrt(msq + EPS) * cg_ref[:, col:col + LANES]
            elif kind == "scale":
                blk = blk * Q_FOLD
            o_ref[:, col:col + LANES] = blk.astype(BF16)


def _proj_call(h, gain, w_bf, col_gain, group_mean):
    n_tok, d_model = h.shape
    n_out = w_bf.shape[1]
    tm = ROW_TILE
    return pl.pallas_call(
        functools.partial(_proj_kernel, d_model=d_model, col_chunk=512),
        grid=(n_tok // tm,),
        in_specs=[pl.BlockSpec((tm, d_model), lambda i: (i, 0)),
                  pl.BlockSpec((1, d_model), lambda i: (0, 0)),
                  pl.BlockSpec((d_model, n_out), lambda i: (0, 0)),
                  pl.BlockSpec((1, n_out), lambda i: (0, 0)),
                  pl.BlockSpec((LANES, LANES), lambda i: (0, 0))],
        out_specs=pl.BlockSpec((tm, n_out), lambda i: (i, 0)),
        out_shape=jax.ShapeDtypeStruct((n_tok, n_out), BF16),
        compiler_params=_cparams(1), name="norm_in_proj")(
            h, gain.reshape(1, d_model), w_bf, col_gain, group_mean)


def _stack_halves(q):
    lane = lax.broadcasted_iota(jnp.int32, q.shape, 1)
    zero = jnp.zeros_like(q)
    return jnp.concatenate([jnp.where(lane < HEAD_DIM, q, zero),
                            jnp.where(lane >= HEAD_DIM, q, zero)], axis=0)


def _scores(q2, k):
    return lax.dot_general(q2, k, (((1,), (1,)), ((), ())), preferred_element_type=F32)


def _lane_tile(x, width):
    reps = width // LANES
    return x if reps == 1 else jnp.concatenate([x] * reps, axis=1)


def _lane_fold(x, op):
    out = x[:, :LANES]
    for c in range(LANES, x.shape[1], LANES):
        out = op(out, x[:, c:c + LANES])
    return out


def _rows_tile(ref, j, tk):
    return ref[0, pl.ds(pl.multiple_of(j * tk, tk), tk), :]


def _for_each(n, fn, ways):
    def group(g, c):
        for w in range(ways):
            fn(g * ways + w)
        return c

    lax.fori_loop(0, n // ways, group, 0)
    part = ways // 2
    while part >= 1:
        start = n // (2 * part) * (2 * part)

        def tail(start=start, part=part):
            for w in range(part):
                fn(start + w)

        pl.when(n % (2 * part) >= part)(tail)
        part //= 2


def _merge_halves(y, tq):
    lane = lax.broadcasted_iota(jnp.int32, (tq, LANES), 1)
    return jnp.where(lane < HEAD_DIM, y[:tq], y[tq:])


SOFTPLUS_CLAMP = 126.0
F32_EXP2_UNDERFLOW = 152.0


def _sb_kernel(q_ref, k_ref, v_ref, u_ref, o_ref, acc_ref, skipped_ref, *, tq):
    i = pl.program_id(2)
    q2 = _stack_halves(q_ref[0])
    rows = 2 * tq
    acc_ref[...] = jnp.zeros(acc_ref.shape, F32)
    skipped_ref[...] = jnp.zeros(skipped_ref.shape, F32)

    def tile(j, diagonal):
        z = _scores(q2, _rows_tile(k_ref, j, tq))
        sp = jnp.maximum(jnp.log(1.0 + jnp.exp2(jnp.minimum(z, SOFTPLUS_CLAMP))) * LOG2E, z)
        if diagonal:
            row = lax.broadcasted_iota(jnp.int32, (rows, tq), 0)
            col = lax.broadcasted_iota(jnp.int32, (rows, tq), 1)
            earlier = col < jnp.where(row >= tq, row - tq, row)
            sp = jnp.where(earlier, sp, 0.0)
        hi, lo = _split_bf16(sp)
        suffix = jnp.dot(jnp.concatenate([hi, lo], axis=1), u_ref[...],
                         preferred_element_type=F32)
        skipped = skipped_ref[...]
        w = jnp.exp2(z - suffix - _lane_tile(skipped, tq))
        if diagonal:
            w = jnp.where(earlier, w, 0.0)
        acc_ref[...] += jnp.dot(w.astype(BF16), _rows_tile(v_ref, j, tq),
                                preferred_element_type=F32)
        skipped = skipped + jnp.broadcast_to(suffix[:, :1], skipped.shape)
        skipped_ref[...] = skipped
        return jnp.min(skipped)

    def more(state):
        t, least_skipped = state
        return (t <= i) & (least_skipped < F32_EXP2_UNDERFLOW)

    def step(state):
        t, _ = state
        return t + 1, tile(i - t, False)

    lax.while_loop(more, step, (jnp.int32(1), tile(i, True)))
    o_ref[0] = _merge_halves(acc_ref[...], tq).astype(o_ref.dtype)


BIAS_FAR, BIAS_MAX, BIAS_SPREAD = 0, 1, 2
SAFE_EXP2_RANGE = 80.0


def _diff_kernel(stats_ref, q_ref, k_ref, v_ref, near_ref, lam_ref, subln_ref, o_ref,
                 s_ref, m_ref, mfar_ref, l_ref, acc_ref, knorm_ref, *, tq, lam_init):
    h = pl.program_id(1)
    i = pl.program_id(2)
    q2 = _stack_halves(q_ref[0])
    rows = 2 * tq
    far = stats_ref[BIAS_FAR, h]

    @pl.when(i == 0)
    def _():
        k_sq = k_ref[0].astype(F32)
        k_sq = k_sq * k_sq
        lane = lax.broadcasted_iota(jnp.int32, k_sq.shape, 1)
        for half in range(2):
            in_half = lane >= HEAD_DIM if half else lane < HEAD_DIM
            norm_sq = jnp.sum(jnp.where(in_half, k_sq, 0.0), axis=1, keepdims=True)
            knorm_ref[half:half + 1, :] = jnp.broadcast_to(
                jnp.max(norm_sq, axis=0, keepdims=True), (1, LANES))

    q_sq = q2.astype(F32)
    q_sq = jnp.sum(q_sq * q_sq, axis=1, keepdims=True)
    row = lax.broadcasted_iota(jnp.int32, (rows, 1), 0)
    reach = jnp.sqrt(q_sq * jnp.where(row < tq, knorm_ref[0:1, 0:1], knorm_ref[1:2, 0:1]))
    one_pass = 2.0 * jnp.max(reach) + stats_ref[BIAS_SPREAD, h] <= SAFE_EXP2_RANGE
    l_ref[...] = jnp.zeros(l_ref.shape, F32)
    acc_ref[...] = jnp.zeros(acc_ref.shape, F32)

    def biased_scores(j, bias):
        s = _scores(q2, _rows_tile(k_ref, j, tq))
        if bias is None:
            return s
        return (s.reshape(2, tq, tq) + bias[None]).reshape(rows, tq)

    def accumulate(j, p):
        l_ref[...] += _lane_fold(p, jnp.add)
        acc_ref[...] += jnp.dot(p.astype(BF16), _rows_tile(v_ref, j, tq),
                                preferred_element_type=F32)

    @pl.when(one_pass)
    def _():
        shift = jnp.broadcast_to(reach + stats_ref[BIAS_MAX, h], (rows, LANES))
        m_ref[...] = shift
        mfar_ref[...] = shift - far

        def tile(j, bias, shift_ref):
            accumulate(j, jnp.exp2(biased_scores(j, bias) - _lane_tile(shift_ref[...], tq)))

        tile(i, near_ref[0, 0], m_ref)
        pl.when(i >= 1)(lambda: tile(i - 1, near_ref[0, 1], m_ref))
        _for_each(jnp.maximum(i - 1, 0), lambda j: tile(j, None, mfar_ref), 4)

    @pl.when(jnp.logical_not(one_pass))
    def _():
        m_ref[...] = jnp.full(m_ref.shape, NEG, F32)

        def score(j, bias):
            s = biased_scores(j, bias)
            if bias is None:
                s = s + far
            s_ref[j] = s
            m_ref[...] = jnp.maximum(m_ref[...], _lane_fold(s, jnp.maximum))

        score(i, near_ref[0, 0])
        pl.when(i >= 1)(lambda: score(i - 1, near_ref[0, 1]))
        _for_each(jnp.maximum(i - 1, 0), lambda j: score(j, None), 4)
        m_ref[...] = jnp.broadcast_to(jnp.max(m_ref[...], axis=1, keepdims=True), m_ref.shape)
        _for_each(i + 1, lambda j: accumulate(
            j, jnp.exp2(s_ref[j] - _lane_tile(m_ref[...], tq))), 4)

    y = acc_ref[...] / jnp.sum(l_ref[...], axis=1, keepdims=True)
    lmb = lam_ref[...]
    lam = (jnp.exp(jnp.sum(lmb[0:1] * lmb[1:2], axis=-1, keepdims=True))
           - jnp.exp(jnp.sum(lmb[2:3] * lmb[3:4], axis=-1, keepdims=True)) + lam_init)
    y = y[:tq] - lam * y[tq:]
    ms = jnp.mean(y * y, axis=-1, keepdims=True)
    y = y * lax.rsqrt(ms + EPS) * subln_ref[...] * (1.0 - lam_init)
    o_ref[0] = y.astype(o_ref.dtype)


def _ca_kernel(q_ref, k_ref, v_ref, bias_ref, o_ref, *, tq, n_near):
    i = pl.program_id(2)
    q2 = _stack_halves(q_ref[0])
    tiles = []
    for d in range(n_near):
        j = jnp.maximum(i - d, 0)
        s = _scores(q2, _rows_tile(k_ref, j, tq)) + bias_ref[0, d]
        if d > 0:
            s = s + jnp.where(i >= d, 0.0, NEG)
        tiles.append((j, s))
    m = _lane_fold(tiles[0][1], jnp.maximum)
    for _, s in tiles[1:]:
        m = jnp.maximum(m, _lane_fold(s, jnp.maximum))
    m = _lane_tile(jnp.broadcast_to(jnp.max(m, axis=1, keepdims=True), m.shape), tq)
    l = None
    acc = None
    for j, s in tiles:
        p = jnp.exp2(s - m)
        pv = jnp.dot(p.astype(BF16), _rows_tile(v_ref, j, tq), preferred_element_type=F32)
        psum = _lane_fold(p, jnp.add)
        l = psum if l is None else l + psum
        acc = pv if acc is None else acc + pv
    y = acc / jnp.sum(l, axis=1, keepdims=True)
    o_ref[0] = _merge_halves(y, tq).astype(o_ref.dtype)


def _t5_bucket(rel):
    nb = T5_BUCKETS // 2
    max_exact = nb // 2
    ret = jnp.where(rel > 0, nb, 0)
    n = jnp.abs(rel)
    large = max_exact + (jnp.log(jnp.maximum(n, 1).astype(F32) / max_exact)
                         / math.log(T5_MAX_DIST / max_exact) * (nb - max_exact)).astype(jnp.int32)
    large = jnp.minimum(large, nb - 1)
    return ret + jnp.where(n < max_exact, n, large)


def _toeplitz_tile(bias_of_rel, tq, d):
    span = 2 * tq
    x = jnp.arange(span, dtype=jnp.int32)
    x = jnp.where(x < tq, x, x - span)
    vec = jnp.transpose(bias_of_rel(x - d * tq)).astype(F32) * LOG2E
    flat = jnp.tile(vec, (1, tq))[:, :tq * (span - 1)]
    return flat.reshape(vec.shape[0], tq, span - 1)[:, :, :tq]


def _diff_bias_tables(t5_bias, tq):
    assert tq >= T5_MAX_DIST and tq % CHUNK == 0
    r = jnp.arange(tq)[:, None]
    c = jnp.arange(tq)[None, :]
    bias_of_rel = lambda rel: t5_bias[_t5_bucket(rel)]
    diag = jnp.where((c // CHUNK) <= (r // CHUNK), _toeplitz_tile(bias_of_rel, tq, 0), NEG)
    scaled = t5_bias.astype(F32) * LOG2E
    far = scaled[_t5_bucket(jnp.int32(-2 * tq))]
    stats = jnp.stack([far, jnp.max(scaled, axis=0),
                       jnp.max(scaled, axis=0) - jnp.min(scaled, axis=0)])
    return jnp.stack([diag, _toeplitz_tile(bias_of_rel, tq, 1)], axis=1), stats


def _ca_bias_tables(rel_bias, tq):
    assert (CA_LEFT_CHUNKS * CHUNK) % tq == 0 and tq % CHUNK == 0
    n_near = CA_LEFT_CHUNKS * CHUNK // tq + 1
    r = jnp.arange(tq)[:, None]
    c = jnp.arange(tq)[None, :]
    bias_of_rel = lambda rel: rel_bias[jnp.clip(rel, -CA_REL_CLIP, CA_REL_CLIP) + CA_REL_CLIP]
    tiles = []
    for d in range(n_near):
        gap = d * (tq // CHUNK) + r // CHUNK - c // CHUNK
        tiles.append(jnp.where((gap >= 0) & (gap <= CA_LEFT_CHUNKS),
                               _toeplitz_tile(bias_of_rel, tq, d), NEG))
    t = jnp.stack(tiles, axis=1)
    n_heads = t.shape[0]
    t = t.reshape(n_heads // 2, 2, n_near, tq, tq).transpose(0, 2, 1, 3, 4)
    return t.reshape(n_heads // 2, n_near, 2 * tq, tq), n_near


def _attention_calls(proj, batch, t5_bias, diff_lambda, diff_subln, ca_rel_bias, lam_init):
    n_tok, n_proj = proj.shape
    seq = n_tok // batch
    d_model = n_proj // 3
    sb_w = d_model // 4
    proj3 = proj.reshape(batch, seq, n_proj)
    tq = ATT_TILE
    nq = seq // tq
    sb_blocks = sb_w // LANES
    df_heads = (d_model // 2) // LANES
    sb_q0 = 0
    df_q0 = 3 * sb_blocks
    ca_q0 = df_q0 + 3 * df_heads

    def qkv_specs(q0, stride):
        return [pl.BlockSpec((1, tq, LANES), lambda b, h, i: (b, i, q0 + h)),
                pl.BlockSpec((1, seq, LANES), lambda b, h, i: (b, 0, q0 + stride + h)),
                pl.BlockSpec((1, seq, LANES), lambda b, h, i: (b, 0, q0 + 2 * stride + h))]

    out_spec = pl.BlockSpec((1, tq, LANES), lambda b, h, i: (b, i, h))
    stacked = pltpu.VMEM((2 * tq, LANES), F32)

    lower = (jnp.arange(tq)[:, None] >= jnp.arange(tq)[None, :]).astype(BF16)
    u = jnp.concatenate([lower, lower], axis=0)
    ya = pl.pallas_call(
        functools.partial(_sb_kernel, tq=tq),
        grid=(batch, sb_blocks, nq),
        in_specs=qkv_specs(sb_q0, sb_blocks) + [pl.BlockSpec(u.shape, lambda b, h, i: (0, 0))],
        out_specs=out_spec,
        out_shape=jax.ShapeDtypeStruct((batch, seq, sb_w), BF16),
        scratch_shapes=[stacked, stacked],
        compiler_params=_cparams(3), name="stick_breaking_attn")(proj3, proj3, proj3, u)

    near, bias_stats = _diff_bias_tables(t5_bias, tq)
    yd = pl.pallas_call(
        functools.partial(_diff_kernel, tq=tq, lam_init=lam_init),
        grid=(batch, df_heads, nq),
        in_specs=[pl.BlockSpec(memory_space=pltpu.SMEM)] + qkv_specs(df_q0, df_heads) + [
            pl.BlockSpec((1, 2, tq, tq), lambda b, h, i: (h, 0, 0, 0)),
            pl.BlockSpec((4, HEAD_DIM), lambda b, h, i: (0, 0)),
            pl.BlockSpec((1, LANES), lambda b, h, i: (0, 0))],
        out_specs=out_spec,
        out_shape=jax.ShapeDtypeStruct((batch, seq, d_model // 2), BF16),
        scratch_shapes=[pltpu.VMEM((nq, 2 * tq, tq), F32), stacked, stacked, stacked, stacked,
                        pltpu.VMEM((8, LANES), F32)],
        compiler_params=_cparams(3), name="differential_attn")(
            bias_stats, proj3, proj3, proj3, near, diff_lambda.astype(F32),
            diff_subln.astype(F32).reshape(1, LANES))

    ca_bias, n_near = _ca_bias_tables(ca_rel_bias, tq)
    yc = pl.pallas_call(
        functools.partial(_ca_kernel, tq=tq, n_near=n_near),
        grid=(batch, sb_blocks, nq),
        in_specs=qkv_specs(ca_q0, sb_blocks) + [
            pl.BlockSpec((1, n_near, 2 * tq, tq), lambda b, h, i: (h, 0, 0, 0))],
        out_specs=out_spec,
        out_shape=jax.ShapeDtypeStruct((batch, seq, sb_w), BF16),
        compiler_params=_cparams(3), name="chunked_rel_attn")(proj3, proj3, proj3, ca_bias)

    return (ya.reshape(n_tok, sb_w), yd.reshape(n_tok, d_model // 2), yc.reshape(n_tok, sb_w))


META_E, META_GATE, META_RANK = 0, 2, 4
META_ROWS = 8


def _row_min_lane(mask, lane_f):
    return jnp.min(jnp.where(mask, lane_f, float(LANES)), axis=1, keepdims=True)


def _out_proj_kernel(h_ref, ya_ref, yd_ref, yc_ref, wo_ref, g_ref, rhi_ref, rlo_ref, rb_ref,
                     ltri_ref, h2_ref, f_ref, meta_ref, meta_t_ref, count_ref, run_ref):
    wa = ya_ref.shape[1]
    wd = yd_ref.shape[1]
    h2 = (h_ref[...]
          + jnp.dot(ya_ref[...], wo_ref[0:wa, :], preferred_element_type=F32)
          + jnp.dot(yd_ref[...], wo_ref[wa:wa + wd, :], preferred_element_type=F32)
          + jnp.dot(yc_ref[...], wo_ref[wa + wd:, :], preferred_element_type=F32))
    h2_ref[...] = h2
    ms = jnp.mean(h2 * h2, axis=-1, keepdims=True)
    f = h2 * lax.rsqrt(ms + EPS) * g_ref[...]
    f_ref[...] = f
    f_hi, f_lo = _split_bf16(f)
    logits = (jnp.dot(f_hi, rhi_ref[...], preferred_element_type=F32)
              + jnp.dot(f_lo, rhi_ref[...], preferred_element_type=F32)
              + jnp.dot(f_hi, rlo_ref[...], preferred_element_type=F32)
              + rb_ref[...])

    lane = lax.broadcasted_iota(jnp.int32, logits.shape, 1)
    lane_f = lane.astype(F32)
    is_group = lane < N_GROUPS
    gl = jnp.where(is_group, logits, NEG)
    g_max = jnp.max(gl, axis=1, keepdims=True)
    g_sel = _row_min_lane(gl == g_max, lane_f)
    g_w = 1.0 / jnp.sum(jnp.where(is_group, jnp.exp(logits - g_max), 0.0), axis=1, keepdims=True)
    first_lane = float(N_GROUPS) + float(EXPERTS_PER_GROUP) * g_sel
    in_group = (lane_f >= first_lane) & (lane_f < first_lane + float(EXPERTS_PER_GROUP))
    el = jnp.where(in_group, logits, NEG)
    m1 = jnp.max(el, axis=1, keepdims=True)
    i1 = _row_min_lane(el == m1, lane_f)
    el = jnp.where(lane_f == i1, NEG, el)
    m2 = jnp.max(el, axis=1, keepdims=True)
    i2 = _row_min_lane(el == m2, lane_f)
    r = jnp.exp(m2 - m1)
    p1 = 1.0 / (1.0 + r)
    gate1 = g_w * p1
    gate2 = g_w * (r * p1)

    @pl.when(pl.program_id(0) == 0)
    def _():
        run_ref[...] = jnp.zeros(run_ref.shape, F32)

    hit1 = lane_f == i1
    hit2 = lane_f == i2
    one_hot = jnp.where(hit1 | hit2, 1.0, 0.0)
    pos = run_ref[...] + jnp.dot(ltri_ref[...], one_hot.astype(BF16), preferred_element_type=F32)
    rank1 = jnp.sum(jnp.where(hit1, pos, 0.0), axis=1, keepdims=True)
    rank2 = jnp.sum(jnp.where(hit2, pos, 0.0), axis=1, keepdims=True)
    run_ref[...] += jnp.sum(one_hot, axis=0, keepdims=True)
    count_ref[...] = run_ref[...]

    meta = jnp.zeros(logits.shape, F32)
    for at, val in ((META_E, i1 - N_GROUPS), (META_E + 1, i2 - N_GROUPS), (META_GATE, gate1),
                    (META_GATE + 1, gate2), (META_RANK, rank1), (META_RANK + 1, rank2)):
        meta = jnp.where(lane == at, val, meta)
    meta_ref[...] = meta
    meta_t_ref[...] = jnp.transpose(meta)[:META_ROWS]


def _out_proj_call(h, ya, yd, yc, wo_bf, gain, r_hi, r_lo, r_bias):
    n_tok, d_model = h.shape
    tm = ROW_TILE
    row = lambda w: pl.BlockSpec((tm, w), lambda i: (i, 0))
    full = lambda a: pl.BlockSpec(a.shape, lambda i: (0, 0))
    gain = gain.reshape(1, d_model)
    ltri = (jnp.arange(tm)[:, None] > jnp.arange(tm)[None, :]).astype(BF16)
    return pl.pallas_call(
        _out_proj_kernel,
        grid=(n_tok // tm,),
        in_specs=[row(d_model), row(ya.shape[1]), row(yd.shape[1]), row(yc.shape[1]),
                  full(wo_bf), full(gain), full(r_hi), full(r_lo), full(r_bias), full(ltri)],
        out_specs=[row(d_model), row(d_model), row(LANES),
                   pl.BlockSpec((META_ROWS, tm), lambda i: (0, i)),
                   pl.BlockSpec((1, LANES), lambda i: (0, 0))],
        out_shape=[jax.ShapeDtypeStruct((n_tok, d_model), F32),
                   jax.ShapeDtypeStruct((n_tok, d_model), F32),
                   jax.ShapeDtypeStruct((n_tok, LANES), F32),
                   jax.ShapeDtypeStruct((META_ROWS, n_tok), F32),
                   jax.ShapeDtypeStruct((1, LANES), F32)],
        scratch_shapes=[pltpu.VMEM((1, LANES), F32)],
        compiler_params=_cparams(1), name="out_proj_router")(
            h, ya, yd, yc, wo_bf, gain, r_hi, r_lo, r_bias, ltri)


def _dispatch_plan(meta_t, count_row, tb, tm):
    n_tok = meta_t.shape[1]
    counts = count_row[0, N_GROUPS:N_GROUPS + N_EXPERTS].astype(jnp.int32)
    padded = (counts + tb - 1) // tb * tb
    e = jnp.arange(N_EXPERTS)
    pend = jnp.sum(jnp.where(e[:, None] >= e[None, :], padded[None, :], 0), axis=1)
    pstart = pend - padded
    n_rows = n_tok * TOP_K_INNER + N_EXPERTS * tb
    block_start = jnp.arange(n_rows // tb, dtype=jnp.int32) * tb
    blk_expert = jnp.minimum(jnp.sum((pend[None, :] <= block_start[:, None]).astype(jnp.int32), axis=1),
                             N_EXPERTS - 1)
    n_used = pend[-1:] // tb
    expert = meta_t[META_E:META_E + TOP_K_INNER].astype(jnp.int32)
    rank = meta_t[META_RANK:META_RANK + TOP_K_INNER].astype(jnp.int32)
    seg_start = jnp.sum(jnp.where(expert[..., None] == e, pstart, 0), axis=-1)
    idx = jnp.transpose(seg_start + rank).reshape(n_tok // tm, 1, tm * TOP_K_INNER)
    return pend.astype(jnp.int32), blk_expert, n_used, idx, n_rows


def _for_each_assignment(tm, fn):
    def body(r, c):
        for k in range(TOP_K_INNER):
            fn(r, k)
        return c

    lax.fori_loop(0, tm, body, 0, unroll=8)


def _dispatch_kernel(pend_ref, idx_ref, f_ref, xs_ref, zero_ref, row_sem, zero_sem, *, tb):
    tm = f_ref.shape[0]

    @pl.when(pl.program_id(0) == 0)
    def _():
        zero_ref[...] = jnp.zeros(zero_ref.shape, F32)

        def block_copy(start):
            return pltpu.make_async_copy(zero_ref, xs_ref.at[pl.ds(pl.multiple_of(start, tb), tb)],
                                         zero_sem)

        for e in range(N_EXPERTS):
            block_copy(jnp.maximum(pend_ref[e] - tb, 0)).start()
        for e in range(N_EXPERTS):
            block_copy(jnp.maximum(pend_ref[e] - tb, 0)).wait()

        def fill(b, c):
            block_copy(b * tb).start()
            block_copy(b * tb).wait()
            return c

        first_unused = lax.shift_right_logical(pend_ref[N_EXPERTS - 1], tb.bit_length() - 1)
        lax.fori_loop(first_unused, xs_ref.shape[0] // tb, fill, 0)

    def row_copy(r, k):
        return pltpu.make_async_copy(f_ref.at[pl.ds(r, 1)], xs_ref.at[pl.ds(idx_ref[0, 0, r * TOP_K_INNER + k], 1)],
                                     row_sem)

    _for_each_assignment(tm, lambda r, k: row_copy(r, k).start())
    _for_each_assignment(tm, lambda r, k: row_copy(r, k).wait())


def _dispatch_call(f, pend, idx, n_rows):
    n_tok, d_model = f.shape
    tm = idx.shape[2] // TOP_K_INNER
    tb = MOE_TILE
    assert tb & (tb - 1) == 0
    grid_spec = pltpu.PrefetchScalarGridSpec(
        num_scalar_prefetch=1,
        grid=(n_tok // tm,),
        in_specs=[pl.BlockSpec((1,) + idx.shape[1:], lambda i, pe: (i, 0, 0),
                               memory_space=pltpu.SMEM),
                  pl.BlockSpec((tm, d_model), lambda i, pe: (i, 0))],
        out_specs=pl.BlockSpec(memory_space=pl.ANY),
        scratch_shapes=[pltpu.VMEM((tb, d_model), F32), pltpu.SemaphoreType.DMA(()),
                        pltpu.SemaphoreType.DMA(())])
    return pl.pallas_call(
        functools.partial(_dispatch_kernel, tb=tb), grid_spec=grid_spec,
        out_shape=jax.ShapeDtypeStruct((n_rows, d_model), F32),
        compiler_params=_cparams(1), name="moe_dispatch")(pend, idx, f)


def _expert_kernel(blk_expert_ref, n_used_ref, x_ref, wg_ref, wu_ref, wd_ref, y_ref,
                   wg_bf, wu_bf, wd_bf):
    i = pl.program_id(0)
    used = i < n_used_ref[0]
    e = blk_expert_ref[i]
    e_prev = blk_expert_ref[jnp.maximum(i - 1, 0)]

    @pl.when(used & ((i == 0) | (e != e_prev)))
    def _():
        wg_bf[...] = wg_ref[0, 0].astype(BF16)
        wu_bf[...] = wu_ref[0, 0].astype(BF16)
        wd_bf[...] = wd_ref[0, 0].astype(BF16)

    @pl.when(used)
    def _():
        x = x_ref[...].astype(BF16)
        g = jnp.dot(x, wg_bf[...], preferred_element_type=F32)
        u = jnp.dot(x, wu_bf[...], preferred_element_type=F32)
        hid = (g / (1.0 + jnp.exp(-g)) * u).astype(BF16)
        y_ref[...] = jnp.dot(hid, wd_bf[...], preferred_element_type=F32)

    @pl.when(jnp.logical_not(used))
    def _():
        y_ref[...] = jnp.zeros(y_ref.shape, F32)


def _expert_call(xs, blk_expert, n_used, w_gate, w_up, w_down, layer):
    n_rows, d_model = xs.shape
    d_exp = w_gate.shape[3]
    tb = MOE_TILE
    blk = lambda i, be, nu: jnp.minimum(i, nu[0] - 1)
    weight = lambda shape: pl.BlockSpec(
        (1, 1) + shape, lambda i, be, nu: (layer, be[blk(i, be, nu)], 0, 0))
    grid_spec = pltpu.PrefetchScalarGridSpec(
        num_scalar_prefetch=2,
        grid=(n_rows // tb,),
        in_specs=[pl.BlockSpec((tb, d_model), lambda i, be, nu: (blk(i, be, nu), 0)),
                  weight((d_model, d_exp)), weight((d_model, d_exp)), weight((d_exp, d_model))],
        out_specs=pl.BlockSpec((tb, d_model), lambda i, be, nu: (i, 0)),
        scratch_shapes=[pltpu.VMEM((d_model, d_exp), BF16), pltpu.VMEM((d_model, d_exp), BF16),
                        pltpu.VMEM((d_exp, d_model), BF16)])
    return pl.pallas_call(
        _expert_kernel, grid_spec=grid_spec,
        out_shape=jax.ShapeDtypeStruct(xs.shape, F32),
        compiler_params=_cparams(1), name="expert_mlp")(
            blk_expert, n_used, xs, w_gate, w_up, w_down)


def _combine_kernel(idx_ref, h_ref, meta_ref, y_ref, o_ref, ybuf_ref, sem):
    tm = h_ref.shape[0]

    def row_copy(r, k):
        return pltpu.make_async_copy(y_ref.at[pl.ds(idx_ref[0, 0, r * TOP_K_INNER + k], 1)],
                                     ybuf_ref.at[k, pl.ds(r, 1)], sem)

    _for_each_assignment(tm, lambda r, k: row_copy(r, k).start())
    _for_each_assignment(tm, lambda r, k: row_copy(r, k).wait())
    meta = meta_ref[...]
    out = h_ref[...]
    for k in range(TOP_K_INNER):
        out = out + meta[:, META_GATE + k:META_GATE + k + 1] * ybuf_ref[k]
    o_ref[...] = out


def _combine_call(h2, meta, y_rows, idx):
    n_tok, d_model = h2.shape
    tm = idx.shape[2] // TOP_K_INNER
    return pl.pallas_call(
        _combine_kernel,
        grid=(n_tok // tm,),
        in_specs=[pl.BlockSpec((1,) + idx.shape[1:], lambda i: (i, 0, 0), memory_space=pltpu.SMEM),
                  pl.BlockSpec((tm, d_model), lambda i: (i, 0)),
                  pl.BlockSpec((tm, LANES), lambda i: (i, 0)),
                  pl.BlockSpec(memory_space=pl.ANY)],
        out_specs=pl.BlockSpec((tm, d_model), lambda i: (i, 0)),
        out_shape=jax.ShapeDtypeStruct((n_tok, d_model), F32),
        scratch_shapes=[pltpu.VMEM((TOP_K_INNER, tm, d_model), F32), pltpu.SemaphoreType.DMA(())],
        compiler_params=_cparams(1), name="moe_combine")(idx, h2, meta, y_rows)


def _moe(h2, f, meta, meta_t, count_row, w_gate, w_up, w_down, layer):
    pend, blk_expert, n_used, idx, n_rows = _dispatch_plan(meta_t, count_row, MOE_TILE, ROW_TILE)
    xs = _dispatch_call(f, pend, idx, n_rows)
    y_rows = _expert_call(xs, blk_expert, n_used, w_gate, w_up, w_down, layer)
    return _combine_call(h2, meta, y_rows, idx)


def _proj_col_gain(diff_qk_norm, ca_qk_norm, d_model):
    sb_w, df_w = d_model // 4, d_model // 2
    ones = lambda n: jnp.ones((n,), F32)
    rep = lambda g, n: jnp.tile(g.astype(F32), n // HEAD_DIM)
    return jnp.concatenate([
        ones(3 * sb_w),
        rep(diff_qk_norm[0], df_w) * Q_FOLD, rep(diff_qk_norm[1], df_w), ones(df_w),
        rep(ca_qk_norm[0], sb_w) * Q_FOLD, rep(ca_qk_norm[1], sb_w), ones(sb_w)]).reshape(1, -1)


def kernel(x, t5_bias, attn_norm, w_in, diff_qk_norm, diff_lambda, diff_subln, ca_qk_norm,
           ca_rel_bias, w_out, ffn_norm, router_group_w, router_group_b, router_expert_w,
           router_expert_b, expert_w_gate, expert_w_up, expert_w_down):
    batch, seq, d_model = x.shape
    depth = w_in.shape[0]
    n_tok = batch * seq
    lane_group = jnp.arange(LANES) // HEAD_DIM
    group_mean = ((lane_group[:, None] == lane_group[None, :]).astype(F32) / HEAD_DIM).astype(BF16)

    h = x.reshape(n_tok, d_model)
    for l in range(depth):
        lam_init = 0.8 - 0.6 * math.exp(-0.3 * l)
        col_gain = _proj_col_gain(diff_qk_norm[l], ca_qk_norm[l], d_model)
        proj = _proj_call(h, attn_norm[l], w_in[l].astype(BF16), col_gain, group_mean)
        ya, yd, yc = _attention_calls(proj, batch, t5_bias, diff_lambda[l], diff_subln[l],
                                      ca_rel_bias[l], lam_init)
        r_w = jnp.concatenate([router_group_w[l], router_expert_w[l]], axis=1).astype(F32)
        r_w = jnp.pad(r_w, ((0, 0), (0, LANES - r_w.shape[1])))
        r_hi, r_lo = _split_bf16(r_w)
        r_b = jnp.concatenate([router_group_b[l], router_expert_b[l]]).astype(F32)
        r_b = jnp.pad(r_b, (0, LANES - r_b.shape[0])).reshape(1, LANES)
        h2, f, meta, meta_t, count_row = _out_proj_call(h, ya, yd, yc, w_out[l].astype(BF16),
                                                        ffn_norm[l], r_hi, r_lo, r_b)
        h = _moe(h2, f, meta, meta_t, count_row, expert_w_gate, expert_w_up, expert_w_down, l)
    return h.reshape(batch, seq, d_model)
```

```python
import functools
import math

import jax
import jax.numpy as jnp
from jax import lax
from jax.experimental import pallas as pl
from jax.experimental.pallas import tpu as pltpu

F32 = jnp.float32
BF16 = jnp.bfloat16

LANES = 128
HEAD_DIM = 64
CHUNK = 64
CA_LEFT_CHUNKS = 8
CA_REL_CLIP = 128
T5_BUCKETS = 32
T5_MAX_DIST = 256
N_GROUPS = 4
EXPERTS_PER_GROUP = 8
N_EXPERTS = N_GROUPS * EXPERTS_PER_GROUP
TOP_K_INNER = 2
EPS = 1e-6
NEG = -1e30
LOG2E = math.log2(math.e)
Q_FOLD = HEAD_DIM ** -0.5 * LOG2E

ROW_TILE = 512
ATT_TILE = 256
MOE_TILE = 512
VMEM_LIMIT = 48 * 1024 * 1024


def _cparams(n_axes):
    return pltpu.CompilerParams(dimension_semantics=("arbitrary",) * n_axes,
                                vmem_limit_bytes=VMEM_LIMIT)


def _split_bf16(x):
    hi = x.astype(BF16)
    lo = (x - hi.astype(F32)).astype(BF16)
    return hi, lo


def _col_kind(col, d_model):
    sb_w, df_w = d_model // 4, d_model // 2
    sb_end = 3 * sb_w
    df_end = sb_end + 3 * df_w
    if col < sb_w:
        return "scale"
    if col < sb_end:
        return "plain"
    if col < sb_end + 2 * df_w:
        return "norm"
    if col < df_end:
        return "plain"
    if col < df_end + 2 * sb_w:
        return "norm"
    return "plain"


def _proj_kernel(h_ref, g_ref, w_ref, cg_ref, gm_ref, o_ref, *, d_model, col_chunk):
    x = h_ref[...]
    ms = jnp.mean(x * x, axis=-1, keepdims=True)
    xn = (x * lax.rsqrt(ms + EPS) * g_ref[...]).astype(BF16)
    n_out = w_ref.shape[1]
    gm = gm_ref[...]
    for c0 in range(0, n_out, col_chunk):
        acc = jnp.dot(xn, w_ref[:, c0:c0 + col_chunk], preferred_element_type=F32)
        for s0 in range(0, col_chunk, LANES):
            col = c0 + s0
            blk = acc[:, s0:s0 + LANES]
            kind = _col_kind(col, d_model)
            if kind == "norm":
                hi, lo = _split_bf16(blk * blk)
                msq = (jnp.dot(hi, gm, preferred_element_type=F32)
                       + jnp.dot(lo, gm, preferred_element_type=F32))
                blk = blk * lax.rsqrt(msq + EPS) * cg_ref[:, col:col + LANES]
            elif kind == "scale":
                blk = blk * Q_FOLD
            o_ref[:, col:col + LANES] = blk.astype(BF16)


def _proj_call(h, gain, w_bf, col_gain, group_mean):
    n_tok, d_model = h.shape
    n_out = w_bf.shape[1]
    tm = ROW_TILE
    return pl.pallas_call(
        functools.partial(_proj_kernel, d_model=d_model, col_chunk=512),
        grid=(n_tok // tm,),
        in_specs=[pl.BlockSpec((tm, d_model), lambda i: (i, 0)),
                  pl.BlockSpec((1, d_model), lambda i: (0, 0)),
                  pl.BlockSpec((d_model, n_out), lambda i: (0, 0)),
                  pl.BlockSpec((1, n_out), lambda i: (0, 0)),
                  pl.BlockSpec((LANES, LANES), lambda i: (0, 0))],
        out_specs=pl.BlockSpec((tm, n_out), lambda i: (i, 0)),
        out_shape=jax.ShapeDtypeStruct((n_tok, n_out), BF16),
        compiler_params=_cparams(1), name="norm_in_proj")(
            h, gain.reshape(1, d_model), w_bf, col_gain, group_mean)


def _stack_halves(q):
    lane = lax.broadcasted_iota(jnp.int32, q.shape, 1)
    zero = jnp.zeros_like(q)
    return jnp.concatenate([jnp.where(lane < HEAD_DIM, q, zero),
                            jnp.where(lane >= HEAD_DIM, q, zero)], axis=0)


def _scores(q2, k):
    return lax.dot_general(q2, k, (((1,), (1,)), ((), ())), preferred_element_type=F32)


def _lane_tile(x, width):
    reps = width // LANES
    return x if reps == 1 else jnp.concatenate([x] * reps, axis=1)


def _lane_fold(x, op):
    out = x[:, :LANES]
    for c in range(LANES, x.shape[1], LANES):
        out = op(out, x[:, c:c + LANES])
    return out


def _rows_tile(ref, j, tk):
    return ref[0, pl.ds(pl.multiple_of(j * tk, tk), tk), :]


def _for_each(n, fn, ways):
    def group(g, c):
        for w in range(ways):
            fn(g * ways + w)
        return c

    lax.fori_loop(0, n // ways, group, 0)
    part = ways // 2
    while part >= 1:
        start = n // (2 * part) * (2 * part)

        def tail(start=start, part=part):
            for w in range(part):
                fn(start + w)

        pl.when(n % (2 * part) >= part)(tail)
        part //= 2


def _merge_halves(y, tq):
    lane = lax.broadcasted_iota(jnp.int32, (tq, LANES), 1)
    return jnp.where(lane < HEAD_DIM, y[:tq], y[tq:])


SOFTPLUS_CLAMP = 126.0
F32_EXP2_UNDERFLOW = 152.0


def _sb_kernel(q_ref, k_ref, v_ref, u_ref, o_ref, acc_ref, skipped_ref, *, tq):
    i = pl.program_id(2)
    q2 = _stack_halves(q_ref[0])
    rows = 2 * tq
    acc_ref[...] = jnp.zeros(acc_ref.shape, F32)
    skipped_ref[...] = jnp.zeros(skipped_ref.shape, F32)

    def tile(j, diagonal):
        z = _scores(q2, _rows_tile(k_ref, j, tq))
        sp = jnp.maximum(jnp.log(1.0 + jnp.exp2(jnp.minimum(z, SOFTPLUS_CLAMP))) * LOG2E, z)
        if diagonal:
            row = lax.broadcasted_iota(jnp.int32, (rows, tq), 0)
            col = lax.broadcasted_iota(jnp.int32, (rows, tq), 1)
            earlier = col < jnp.where(row >= tq, row - tq, row)
            sp = jnp.where(earlier, sp, 0.0)
        hi, lo = _split_bf16(sp)
        suffix = jnp.dot(jnp.concatenate([hi, lo], axis=1), u_ref[...],
                         preferred_element_type=F32)
        skipped = skipped_ref[...]
        w = jnp.exp2(z - suffix - _lane_tile(skipped, tq))
        if diagonal:
            w = jnp.where(earlier, w, 0.0)
        acc_ref[...] += jnp.dot(w.astype(BF16), _rows_tile(v_ref, j, tq),
                                preferred_element_type=F32)
        skipped = skipped + jnp.broadcast_to(suffix[:, :1], skipped.shape)
        skipped_ref[...] = skipped
        return jnp.min(skipped)

    def more(state):
        t, least_skipped = state
        return (t <= i) & (least_skipped < F32_EXP2_UNDERFLOW)

    def step(state):
        t, _ = state
        return t + 1, tile(i - t, False)

    lax.while_loop(more, step, (jnp.int32(1), tile(i, True)))
    o_ref[0] = _merge_halves(acc_ref[...], tq).astype(o_ref.dtype)


BIAS_FAR, LOGIT_MAX, LOGIT_SPREAD = 0, 1, 2
SAFE_EXP2_RANGE = 80.0


def _qk_norm_reach(qk_norm):
    g = jnp.abs(qk_norm.astype(F32))
    return 1.01 * HEAD_DIM * Q_FOLD * jnp.max(g[0]) * jnp.max(g[1])


def _logit_stats(far, bias_max, bias_min, reach):
    return jnp.stack([far, bias_max + reach, bias_max - bias_min + 2.0 * reach])


def _diff_kernel(stats_ref, q_ref, k_ref, v_ref, near_ref, lam_ref, subln_ref, o_ref,
                 s_ref, p_ref, m_ref, l_ref, acc_ref, *, tq, lam_init):
    h = pl.program_id(1)
    i = pl.program_id(2)
    q2 = _stack_halves(q_ref[0])
    rows = 2 * tq
    far = stats_ref[BIAS_FAR, h]
    bounded = stats_ref[LOGIT_SPREAD, h] <= SAFE_EXP2_RANGE
    lmb = lam_ref[...]
    lam = (jnp.exp(jnp.sum(lmb[0:1] * lmb[1:2], axis=-1, keepdims=True))
           - jnp.exp(jnp.sum(lmb[2:3] * lmb[3:4], axis=-1, keepdims=True)) + lam_init)
    l_ref[...] = jnp.zeros(l_ref.shape, F32)
    acc_ref[...] = jnp.zeros(acc_ref.shape, F32)

    def biased_scores(j, bias):
        s = _scores(q2, _rows_tile(k_ref, j, tq))
        if bias is None:
            return s
        return (s.reshape(2, tq, tq) + bias[None]).reshape(rows, tq)

    @pl.when(bounded)
    def _():
        shift = stats_ref[LOGIT_MAX, h]

        def probs(j, bias, shift):
            p = jnp.exp2(biased_scores(j, bias) - shift)
            l_ref[...] += _lane_fold(p, jnp.add)
            p_ref[j] = p.astype(BF16)

        probs(i, near_ref[0, 0], shift)
        pl.when(i >= 1)(lambda: probs(i - 1, near_ref[0, 1], shift))
        _for_each(jnp.maximum(i - 1, 0), lambda j: probs(j, None, shift - far), 4)

        inv_l = 1.0 / jnp.sum(l_ref[...], axis=1, keepdims=True)
        m_ref[0:tq] = jnp.broadcast_to(inv_l[:tq], (tq, LANES))
        m_ref[tq:rows] = jnp.broadcast_to(-lam * inv_l[tq:], (tq, LANES))

        def weigh(j):
            p = p_ref[j]
            w = (p[:tq].astype(F32) * _lane_tile(m_ref[0:tq], tq)
                 + p[tq:].astype(F32) * _lane_tile(m_ref[tq:rows], tq))
            acc_ref[0:tq] += jnp.dot(w.astype(BF16), _rows_tile(v_ref, j, tq),
                                     preferred_element_type=F32)

        _for_each(i + 1, weigh, 4)

    @pl.when(jnp.logical_not(bounded))
    def _():
        m_ref[...] = jnp.full(m_ref.shape, NEG, F32)

        def score(j, bias):
            s = biased_scores(j, bias)
            if bias is None:
                s = s + far
            s_ref[j] = s
            m_ref[...] = jnp.maximum(m_ref[...], _lane_fold(s, jnp.maximum))

        score(i, near_ref[0, 0])
        pl.when(i >= 1)(lambda: score(i - 1, near_ref[0, 1]))
        _for_each(jnp.maximum(i - 1, 0), lambda j: score(j, None), 4)
        m_ref[...] = jnp.broadcast_to(jnp.max(m_ref[...], axis=1, keepdims=True), m_ref.shape)

        def weigh(j):
            p = jnp.exp2(s_ref[j] - _lane_tile(m_ref[...], tq))
            l_ref[...] += _lane_fold(p, jnp.add)
            acc_ref[...] += jnp.dot(p.astype(BF16), _rows_tile(v_ref, j, tq),
                                    preferred_element_type=F32)

        _for_each(i + 1, weigh, 4)
        y = acc_ref[...] / jnp.sum(l_ref[...], axis=1, keepdims=True)
        acc_ref[0:tq] = y[:tq] - lam * y[tq:]

    y = acc_ref[0:tq]
    ms = jnp.mean(y * y, axis=-1, keepdims=True)
    y = y * lax.rsqrt(ms + EPS) * subln_ref[...] * (1.0 - lam_init)
    o_ref[0] = y.astype(o_ref.dtype)


def _ca_kernel(stats_ref, q_ref, k_ref, v_ref, bias_ref, o_ref, *, tq, n_near):
    h = pl.program_id(1)
    i = pl.program_id(2)
    q2 = _stack_halves(q_ref[0])
    bounded = stats_ref[LOGIT_SPREAD, h] <= SAFE_EXP2_RANGE

    def scores(d):
        j = jnp.maximum(i - d, 0)
        s = _scores(q2, _rows_tile(k_ref, j, tq)) + bias_ref[0, d]
        if d > 0:
            s = s + jnp.where(i >= d, 0.0, NEG)
        return j, s

    def attend(tiles, shift):
        l = None
        acc = None
        for j, s in tiles:
            p = jnp.exp2(s - shift)
            pv = jnp.dot(p.astype(BF16), _rows_tile(v_ref, j, tq), preferred_element_type=F32)
            psum = _lane_fold(p, jnp.add)
            l = psum if l is None else l + psum
            acc = pv if acc is None else acc + pv
        y = acc / jnp.sum(l, axis=1, keepdims=True)
        o_ref[0] = _merge_halves(y, tq).astype(o_ref.dtype)

    @pl.when(bounded)
    def _():
        attend((scores(d) for d in range(n_near)), stats_ref[LOGIT_MAX, h])

    @pl.when(jnp.logical_not(bounded))
    def _():
        tiles = [scores(d) for d in range(n_near)]
        m = _lane_fold(tiles[0][1], jnp.maximum)
        for _, s in tiles[1:]:
            m = jnp.maximum(m, _lane_fold(s, jnp.maximum))
        m = jnp.broadcast_to(jnp.max(m, axis=1, keepdims=True), m.shape)
        attend(tiles, _lane_tile(m, tq))


def _t5_bucket(rel):
    nb = T5_BUCKETS // 2
    max_exact = nb // 2
    ret = jnp.where(rel > 0, nb, 0)
    n = jnp.abs(rel)
    large = max_exact + (jnp.log(jnp.maximum(n, 1).astype(F32) / max_exact)
                         / math.log(T5_MAX_DIST / max_exact) * (nb - max_exact)).astype(jnp.int32)
    large = jnp.minimum(large, nb - 1)
    return ret + jnp.where(n < max_exact, n, large)


def _toeplitz_tile(bias_of_rel, tq, d):
    span = 2 * tq
    x = jnp.arange(span, dtype=jnp.int32)
    x = jnp.where(x < tq, x, x - span)
    vec = jnp.transpose(bias_of_rel(x - d * tq)).astype(F32) * LOG2E
    flat = jnp.tile(vec, (1, tq))[:, :tq * (span - 1)]
    return flat.reshape(vec.shape[0], tq, span - 1)[:, :, :tq]


def _diff_bias_tables(t5_bias, qk_norm, tq):
    assert tq >= T5_MAX_DIST and tq % CHUNK == 0
    r = jnp.arange(tq)[:, None]
    c = jnp.arange(tq)[None, :]
    bias_of_rel = lambda rel: t5_bias[_t5_bucket(rel)]
    diag = jnp.where((c // CHUNK) <= (r // CHUNK), _toeplitz_tile(bias_of_rel, tq, 0), NEG)
    scaled = t5_bias.astype(F32) * LOG2E
    far = scaled[_t5_bucket(jnp.int32(-2 * tq))]
    stats = _logit_stats(far, jnp.max(scaled, axis=0), jnp.min(scaled, axis=0),
                         _qk_norm_reach(qk_norm))
    return jnp.stack([diag, _toeplitz_tile(bias_of_rel, tq, 1)], axis=1), stats


def _ca_bias_tables(rel_bias, qk_norm, tq):
    assert (CA_LEFT_CHUNKS * CHUNK) % tq == 0 and tq % CHUNK == 0
    n_near = CA_LEFT_CHUNKS * CHUNK // tq + 1
    r = jnp.arange(tq)[:, None]
    c = jnp.arange(tq)[None, :]
    bias_of_rel = lambda rel: rel_bias[jnp.clip(rel, -CA_REL_CLIP, CA_REL_CLIP) + CA_REL_CLIP]
    tiles = []
    for d in range(n_near):
        gap = d * (tq // CHUNK) + r // CHUNK - c // CHUNK
        tiles.append(jnp.where((gap >= 0) & (gap <= CA_LEFT_CHUNKS),
                               _toeplitz_tile(bias_of_rel, tq, d), NEG))
    t = jnp.stack(tiles, axis=1)
    n_heads = t.shape[0]
    t = t.reshape(n_heads // 2, 2, n_near, tq, tq).transpose(0, 2, 1, 3, 4)
    scaled = (rel_bias.astype(F32) * LOG2E).reshape(rel_bias.shape[0], n_heads // 2, 2)
    top = jnp.max(scaled, axis=(0, 2))
    stats = _logit_stats(jnp.zeros_like(top), top, jnp.min(scaled, axis=(0, 2)),
                         _qk_norm_reach(qk_norm))
    return t.reshape(n_heads // 2, n_near, 2 * tq, tq), stats, n_near


def _attention_calls(proj, batch, t5_bias, diff_qk_norm, diff_lambda, diff_subln, ca_qk_norm,
                     ca_rel_bias, lam_init):
    n_tok, n_proj = proj.shape
    seq = n_tok // batch
    d_model = n_proj // 3
    sb_w = d_model // 4
    proj3 = proj.reshape(batch, seq, n_proj)
    tq = ATT_TILE
    nq = seq // tq
    sb_blocks = sb_w // LANES
    df_heads = (d_model // 2) // LANES
    sb_q0 = 0
    df_q0 = 3 * sb_blocks
    ca_q0 = df_q0 + 3 * df_heads

    def qkv_specs(q0, stride):
        return [pl.BlockSpec((1, tq, LANES), lambda b, h, i: (b, i, q0 + h)),
                pl.BlockSpec((1, seq, LANES), lambda b, h, i: (b, 0, q0 + stride + h)),
                pl.BlockSpec((1, seq, LANES), lambda b, h, i: (b, 0, q0 + 2 * stride + h))]

    out_spec = pl.BlockSpec((1, tq, LANES), lambda b, h, i: (b, i, h))
    stacked = pltpu.VMEM((2 * tq, LANES), F32)

    lower = (jnp.arange(tq)[:, None] >= jnp.arange(tq)[None, :]).astype(BF16)
    u = jnp.concatenate([lower, lower], axis=0)
    ya = pl.pallas_call(
        functools.partial(_sb_kernel, tq=tq),
        grid=(batch, sb_blocks, nq),
        in_specs=qkv_specs(sb_q0, sb_blocks) + [pl.BlockSpec(u.shape, lambda b, h, i: (0, 0))],
        out_specs=out_spec,
        out_shape=jax.ShapeDtypeStruct((batch, seq, sb_w), BF16),
        scratch_shapes=[stacked, stacked],
        compiler_params=_cparams(3), name="stick_breaking_attn")(proj3, proj3, proj3, u)

    near, bias_stats = _diff_bias_tables(t5_bias, diff_qk_norm, tq)
    yd = pl.pallas_call(
        functools.partial(_diff_kernel, tq=tq, lam_init=lam_init),
        grid=(batch, df_heads, nq),
        in_specs=[pl.BlockSpec(memory_space=pltpu.SMEM)] + qkv_specs(df_q0, df_heads) + [
            pl.BlockSpec((1, 2, tq, tq), lambda b, h, i: (h, 0, 0, 0)),
            pl.BlockSpec((4, HEAD_DIM), lambda b, h, i: (0, 0)),
            pl.BlockSpec((1, LANES), lambda b, h, i: (0, 0))],
        out_specs=out_spec,
        out_shape=jax.ShapeDtypeStruct((batch, seq, d_model // 2), BF16),
        scratch_shapes=[pltpu.VMEM((nq, 2 * tq, tq), F32), pltpu.VMEM((nq, 2 * tq, tq), BF16),
                        stacked, stacked, stacked],
        compiler_params=_cparams(3), name="differential_attn")(
            bias_stats, proj3, proj3, proj3, near, diff_lambda.astype(F32),
            diff_subln.astype(F32).reshape(1, LANES))

    ca_bias, ca_stats, n_near = _ca_bias_tables(ca_rel_bias, ca_qk_norm, tq)
    yc = pl.pallas_call(
        functools.partial(_ca_kernel, tq=tq, n_near=n_near),
        grid=(batch, sb_blocks, nq),
        in_specs=[pl.BlockSpec(memory_space=pltpu.SMEM)] + qkv_specs(ca_q0, sb_blocks) + [
            pl.BlockSpec((1, n_near, 2 * tq, tq), lambda b, h, i: (h, 0, 0, 0))],
        out_specs=out_spec,
        out_shape=jax.ShapeDtypeStruct((batch, seq, sb_w), BF16),
        compiler_params=_cparams(3), name="chunked_rel_attn")(
            ca_stats, proj3, proj3, proj3, ca_bias)

    return (ya.reshape(n_tok, sb_w), yd.reshape(n_tok, d_model // 2), yc.reshape(n_tok, sb_w))


META_E, META_GATE, META_RANK = 0, 2, 4
META_ROWS = 8


def _row_min_lane(mask, lane_f):
    return jnp.min(jnp.where(mask, lane_f, float(LANES)), axis=1, keepdims=True)


def _out_proj_kernel(h_ref, ya_ref, yd_ref, yc_ref, wo_ref, g_ref, rhi_ref, rlo_ref, rb_ref,
                     ltri_ref, h2_ref, f_ref, meta_ref, meta_t_ref, count_ref, run_ref):
    wa = ya_ref.shape[1]
    wd = yd_ref.shape[1]
    h2 = (h_ref[...]
          + jnp.dot(ya_ref[...], wo_ref[0:wa, :], preferred_element_type=F32)
          + jnp.dot(yd_ref[...], wo_ref[wa:wa + wd, :], preferred_element_type=F32)
          + jnp.dot(yc_ref[...], wo_ref[wa + wd:, :], preferred_element_type=F32))
    h2_ref[...] = h2
    ms = jnp.mean(h2 * h2, axis=-1, keepdims=True)
    f = h2 * lax.rsqrt(ms + EPS) * g_ref[...]
    f_ref[...] = f
    f_hi, f_lo = _split_bf16(f)
    logits = (jnp.dot(f_hi, rhi_ref[...], preferred_element_type=F32)
              + jnp.dot(f_lo, rhi_ref[...], preferred_element_type=F32)
              + jnp.dot(f_hi, rlo_ref[...], preferred_element_type=F32)
              + rb_ref[...])

    lane = lax.broadcasted_iota(jnp.int32, logits.shape, 1)
    lane_f = lane.astype(F32)
    is_group = lane < N_GROUPS
    gl = jnp.where(is_group, logits, NEG)
    g_max = jnp.max(gl, axis=1, keepdims=True)
    g_sel = _row_min_lane(gl == g_max, lane_f)
    g_w = 1.0 / jnp.sum(jnp.where(is_group, jnp.exp(logits - g_max), 0.0), axis=1, keepdims=True)
    first_lane = float(N_GROUPS) + float(EXPERTS_PER_GROUP) * g_sel
    in_group = (lane_f >= first_lane) & (lane_f < first_lane + float(EXPERTS_PER_GROUP))
    el = jnp.where(in_group, logits, NEG)
    m1 = jnp.max(el, axis=1, keepdims=True)
    i1 = _row_min_lane(el == m1, lane_f)
    el = jnp.where(lane_f == i1, NEG, el)
    m2 = jnp.max(el, axis=1, keepdims=True)
    i2 = _row_min_lane(el == m2, lane_f)
    r = jnp.exp(m2 - m1)
    p1 = 1.0 / (1.0 + r)
    gate1 = g_w * p1
    gate2 = g_w * (r * p1)

    @pl.when(pl.program_id(0) == 0)
    def _():
        run_ref[...] = jnp.zeros(run_ref.shape, F32)

    hit1 = lane_f == i1
    hit2 = lane_f == i2
    one_hot = jnp.where(hit1 | hit2, 1.0, 0.0)
    pos = run_ref[...] + jnp.dot(ltri_ref[...], one_hot.astype(BF16), preferred_element_type=F32)
    rank1 = jnp.sum(jnp.where(hit1, pos, 0.0), axis=1, keepdims=True)
    rank2 = jnp.sum(jnp.where(hit2, pos, 0.0), axis=1, keepdims=True)
    run_ref[...] += jnp.sum(one_hot, axis=0, keepdims=True)
    count_ref[...] = run_ref[...]

    meta = jnp.zeros(logits.shape, F32)
    for at, val in ((META_E, i1 - N_GROUPS), (META_E + 1, i2 - N_GROUPS), (META_GATE, gate1),
                    (META_GATE + 1, gate2), (META_RANK, rank1), (META_RANK + 1, rank2)):
        meta = jnp.where(lane == at, val, meta)
    meta_ref[...] = meta
    meta_t_ref[...] = jnp.transpose(meta)[:META_ROWS]


def _out_proj_call(h, ya, yd, yc, wo_bf, gain, r_hi, r_lo, r_bias):
    n_tok, d_model = h.shape
    tm = ROW_TILE
    row = lambda w: pl.BlockSpec((tm, w), lambda i: (i, 0))
    full = lambda a: pl.BlockSpec(a.shape, lambda i: (0, 0))
    gain = gain.reshape(1, d_model)
    ltri = (jnp.arange(tm)[:, None] > jnp.arange(tm)[None, :]).astype(BF16)
    return pl.pallas_call(
        _out_proj_kernel,
        grid=(n_tok // tm,),
        in_specs=[row(d_model), row(ya.shape[1]), row(yd.shape[1]), row(yc.shape[1]),
                  full(wo_bf), full(gain), full(r_hi), full(r_lo), full(r_bias), full(ltri)],
        out_specs=[row(d_model), row(d_model), row(LANES),
                   pl.BlockSpec((META_ROWS, tm), lambda i: (0, i)),
                   pl.BlockSpec((1, LANES), lambda i: (0, 0))],
        out_shape=[jax.ShapeDtypeStruct((n_tok, d_model), F32),
                   jax.ShapeDtypeStruct((n_tok, d_model), F32),
                   jax.ShapeDtypeStruct((n_tok, LANES), F32),
                   jax.ShapeDtypeStruct((META_ROWS, n_tok), F32),
                   jax.ShapeDtypeStruct((1, LANES), F32)],
        scratch_shapes=[pltpu.VMEM((1, LANES), F32)],
        compiler_params=_cparams(1), name="out_proj_router")(
            h, ya, yd, yc, wo_bf, gain, r_hi, r_lo, r_bias, ltri)


def _dispatch_plan(meta_t, count_row, tb, tm):
    n_tok = meta_t.shape[1]
    counts = count_row[0, N_GROUPS:N_GROUPS + N_EXPERTS].astype(jnp.int32)
    padded = (counts + tb - 1) // tb * tb
    e = jnp.arange(N_EXPERTS)
    pend = jnp.sum(jnp.where(e[:, None] >= e[None, :], padded[None, :], 0), axis=1)
    pstart = pend - padded
    n_rows = n_tok * TOP_K_INNER + N_EXPERTS * tb
    block_start = jnp.arange(n_rows // tb, dtype=jnp.int32) * tb
    blk_expert = jnp.minimum(jnp.sum((pend[None, :] <= block_start[:, None]).astype(jnp.int32), axis=1),
                             N_EXPERTS - 1)
    n_used = pend[-1:] // tb
    expert = meta_t[META_E:META_E + TOP_K_INNER].astype(jnp.int32)
    rank = meta_t[META_RANK:META_RANK + TOP_K_INNER].astype(jnp.int32)
    seg_start = jnp.sum(jnp.where(expert[..., None] == e, pstart, 0), axis=-1)
    idx = jnp.transpose(seg_start + rank).reshape(n_tok // tm, 1, tm * TOP_K_INNER)
    return pend.astype(jnp.int32), blk_expert, n_used, idx, n_rows


def _for_each_assignment(tm, fn):
    def body(r, c):
        for k in range(TOP_K_INNER):
            fn(r, k)
        return c

    lax.fori_loop(0, tm, body, 0, unroll=8)


def _dispatch_kernel(pend_ref, idx_ref, f_ref, xs_ref, zero_ref, row_sem, zero_sem, *, tb):
    tm = f_ref.shape[0]

    @pl.when(pl.program_id(0) == 0)
    def _():
        zero_ref[...] = jnp.zeros(zero_ref.shape, F32)

        def block_copy(start):
            return pltpu.make_async_copy(zero_ref, xs_ref.at[pl.ds(pl.multiple_of(start, tb), tb)],
                                         zero_sem)

        for e in range(N_EXPERTS):
            block_copy(jnp.maximum(pend_ref[e] - tb, 0)).start()
        for e in range(N_EXPERTS):
            block_copy(jnp.maximum(pend_ref[e] - tb, 0)).wait()

        def fill(b, c):
            block_copy(b * tb).start()
            block_copy(b * tb).wait()
            return c

        first_unused = lax.shift_right_logical(pend_ref[N_EXPERTS - 1], tb.bit_length() - 1)
        lax.fori_loop(first_unused, xs_ref.shape[0] // tb, fill, 0)

    def row_copy(r, k):
        return pltpu.make_async_copy(f_ref.at[pl.ds(r, 1)], xs_ref.at[pl.ds(idx_ref[0, 0, r * TOP_K_INNER + k], 1)],
                                     row_sem)

    _for_each_assignment(tm, lambda r, k: row_copy(r, k).start())
    _for_each_assignment(tm, lambda r, k: row_copy(r, k).wait())


def _dispatch_call(f, pend, idx, n_rows):
    n_tok, d_model = f.shape
    tm = idx.shape[2] // TOP_K_INNER
    tb = MOE_TILE
    assert tb & (tb - 1) == 0
    grid_spec = pltpu.PrefetchScalarGridSpec(
        num_scalar_prefetch=1,
        grid=(n_tok // tm,),
        in_specs=[pl.BlockSpec((1,) + idx.shape[1:], lambda i, pe: (i, 0, 0),
                               memory_space=pltpu.SMEM),
                  pl.BlockSpec((tm, d_model), lambda i, pe: (i, 0))],
        out_specs=pl.BlockSpec(memory_space=pl.ANY),
        scratch_shapes=[pltpu.VMEM((tb, d_model), F32), pltpu.SemaphoreType.DMA(()),
                        pltpu.SemaphoreType.DMA(())])
    return pl.pallas_call(
        functools.partial(_dispatch_kernel, tb=tb), grid_spec=grid_spec,
        out_shape=jax.ShapeDtypeStruct((n_rows, d_model), F32),
        compiler_params=_cparams(1), name="moe_dispatch")(pend, idx, f)


def _expert_kernel(blk_expert_ref, n_used_ref, x_ref, wg_ref, wu_ref, wd_ref, y_ref,
                   wg_bf, wu_bf, wd_bf):
    i = pl.program_id(0)
    used = i < n_used_ref[0]
    e = blk_expert_ref[i]
    e_prev = blk_expert_ref[jnp.maximum(i - 1, 0)]

    @pl.when(used & ((i == 0) | (e != e_prev)))
    def _():
        wg_bf[...] = wg_ref[0, 0].astype(BF16)
        wu_bf[...] = wu_ref[0, 0].astype(BF16)
        wd_bf[...] = wd_ref[0, 0].astype(BF16)

    @pl.when(used)
    def _():
        x = x_ref[...].astype(BF16)
        g = jnp.dot(x, wg_bf[...], preferred_element_type=F32)
        u = jnp.dot(x, wu_bf[...], preferred_element_type=F32)
        hid = (g / (1.0 + jnp.exp(-g)) * u).astype(BF16)
        y_ref[...] = jnp.dot(hid, wd_bf[...], preferred_element_type=F32)

    @pl.when(jnp.logical_not(used))
    def _():
        y_ref[...] = jnp.zeros(y_ref.shape, F32)


def _expert_call(xs, blk_expert, n_used, w_gate, w_up, w_down, layer):
    n_rows, d_model = xs.shape
    d_exp = w_gate.shape[3]
    tb = MOE_TILE
    blk = lambda i, be, nu: jnp.minimum(i, nu[0] - 1)
    weight = lambda shape: pl.BlockSpec(
        (1, 1) + shape, lambda i, be, nu: (layer, be[blk(i, be, nu)], 0, 0))
    grid_spec = pltpu.PrefetchScalarGridSpec(
        num_scalar_prefetch=2,
        grid=(n_rows // tb,),
        in_specs=[pl.BlockSpec((tb, d_model), lambda i, be, nu: (blk(i, be, nu), 0)),
                  weight((d_model, d_exp)), weight((d_model, d_exp)), weight((d_exp, d_model))],
        out_specs=pl.BlockSpec((tb, d_model), lambda i, be, nu: (i, 0)),
        scratch_shapes=[pltpu.VMEM((d_model, d_exp), BF16), pltpu.VMEM((d_model, d_exp), BF16),
                        pltpu.VMEM((d_exp, d_model), BF16)])
    return pl.pallas_call(
        _expert_kernel, grid_spec=grid_spec,
        out_shape=jax.ShapeDtypeStruct(xs.shape, F32),
        compiler_params=_cparams(1), name="expert_mlp")(
            blk_expert, n_used, xs, w_gate, w_up, w_down)


def _combine_kernel(idx_ref, h_ref, meta_ref, y_ref, o_ref, ybuf_ref, sem):
    tm = h_ref.shape[0]

    def row_copy(r, k):
        return pltpu.make_async_copy(y_ref.at[pl.ds(idx_ref[0, 0, r * TOP_K_INNER + k], 1)],
                                     ybuf_ref.at[k, pl.ds(r, 1)], sem)

    _for_each_assignment(tm, lambda r, k: row_copy(r, k).start())
    _for_each_assignment(tm, lambda r, k: row_copy(r, k).wait())
    meta = meta_ref[...]
    out = h_ref[...]
    for k in range(TOP_K_INNER):
        out = out + meta[:, META_GATE + k:META_GATE + k + 1] * ybuf_ref[k]
    o_ref[...] = out


def _combine_call(h2, meta, y_rows, idx):
    n_tok, d_model = h2.shape
    tm = idx.shape[2] // TOP_K_INNER
    return pl.pallas_call(
        _combine_kernel,
        grid=(n_tok // tm,),
        in_specs=[pl.BlockSpec((1,) + idx.shape[1:], lambda i: (i, 0, 0), memory_space=pltpu.SMEM),
                  pl.BlockSpec((tm, d_model), lambda i: (i, 0)),
                  pl.BlockSpec((tm, LANES), lambda i: (i, 0)),
                  pl.BlockSpec(memory_space=pl.ANY)],
        out_specs=pl.BlockSpec((tm, d_model), lambda i: (i, 0)),
        out_shape=jax.ShapeDtypeStruct((n_tok, d_model), F32),
        scratch_shapes=[pltpu.VMEM((TOP_K_INNER, tm, d_model), F32), pltpu.SemaphoreType.DMA(())],
        compiler_params=_cparams(1), name="moe_combine")(idx, h2, meta, y_rows)


def _moe(h2, f, meta, meta_t, count_row, w_gate, w_up, w_down, layer):
    pend, blk_expert, n_used, idx, n_rows = _dispatch_plan(meta_t, count_row, MOE_TILE, ROW_TILE)
    xs = _dispatch_call(f, pend, idx, n_rows)
    y_rows = _expert_call(xs, blk_expert, n_used, w_gate, w_up, w_down, layer)
    return _combine_call(h2, meta, y_rows, idx)


def _proj_col_gain(diff_qk_norm, ca_qk_norm, d_model):
    sb_w, df_w = d_model // 4, d_model // 2
    ones = lambda n: jnp.ones((n,), F32)
    rep = lambda g, n: jnp.tile(g.astype(F32), n // HEAD_DIM)
    return jnp.concatenate([
        ones(3 * sb_w),
        rep(diff_qk_norm[0], df_w) * Q_FOLD, rep(diff_qk_norm[1], df_w), ones(df_w),
        rep(ca_qk_norm[0], sb_w) * Q_FOLD, rep(ca_qk_norm[1], sb_w), ones(sb_w)]).reshape(1, -1)


def kernel(x, t5_bias, attn_norm, w_in, diff_qk_norm, diff_lambda, diff_subln, ca_qk_norm,
           ca_rel_bias, w_out, ffn_norm, router_group_w, router_group_b, router_expert_w,
           router_expert_b, expert_w_gate, expert_w_up, expert_w_down):
    batch, seq, d_model = x.shape
    depth = w_in.shape[0]
    n_tok = batch * seq
    lane_group = jnp.arange(LANES) // HEAD_DIM
    group_mean = ((lane_group[:, None] == lane_group[None, :]).astype(F32) / HEAD_DIM).astype(BF16)

    h = x.reshape(n_tok, d_model)
    for l in range(depth):
        lam_init = 0.8 - 0.6 * math.exp(-0.3 * l)
        col_gain = _proj_col_gain(diff_qk_norm[l], ca_qk_norm[l], d_model)
        proj = _proj_call(h, attn_norm[l], w_in[l].astype(BF16), col_gain, group_mean)
        ya, yd, yc = _attention_calls(proj, batch, t5_bias, diff_qk_norm[l], diff_lambda[l],
                                      diff_subln[l], ca_qk_norm[l], ca_rel_bias[l], lam_init)
        r_w = jnp.concatenate([router_group_w[l], router_expert_w[l]], axis=1).astype(F32)
        r_w = jnp.pad(r_w, ((0, 0), (0, LANES - r_w.shape[1])))
        r_hi, r_lo = _split_bf16(r_w)
        r_b = jnp.concatenate([router_group_b[l], router_expert_b[l]]).astype(F32)
        r_b = jnp.pad(r_b, (0, LANES - r_b.shape[0])).reshape(1, LANES)
        h2, f, meta, meta_t, count_row = _out_proj_call(h, ya, yd, yc, w_out[l].astype(BF16),
                                                        ffn_norm[l], r_hi, r_lo, r_b)
        h = _moe(h2, f, meta, meta_t, count_row, expert_w_gate, expert_w_up, expert_w_down, l)
    return h.reshape(batch, seq, d_model)
```

```python
import functools
import math

import jax
import jax.numpy as jnp
from jax import lax
from jax.experimental import pallas as pl
from jax.experimental.pallas import tpu as pltpu

F32 = jnp.float32
BF16 = jnp.bfloat16

LANES = 128
HEAD_DIM = 64
CHUNK = 64
CA_LEFT_CHUNKS = 8
CA_REL_CLIP = 128
T5_BUCKETS = 32
T5_MAX_DIST = 256
N_GROUPS = 4
EXPERTS_PER_GROUP = 8
N_EXPERTS = N_GROUPS * EXPERTS_PER_GROUP
TOP_K_INNER = 2
EPS = 1e-6
NEG = -1e30
LOG2E = math.log2(math.e)
Q_FOLD = HEAD_DIM ** -0.5 * LOG2E

ROW_TILE = 512
ATT_TILE = 256
MOE_TILE = 512
VMEM_LIMIT = 48 * 1024 * 1024


def _cparams(n_axes):
    return pltpu.CompilerParams(dimension_semantics=("arbitrary",) * n_axes,
                                vmem_limit_bytes=VMEM_LIMIT)


def _split_bf16(x):
    hi = x.astype(BF16)
    lo = (x - hi.astype(F32)).astype(BF16)
    return hi, lo


def _col_kind(col, d_model):
    sb_w, df_w = d_model // 4, d_model // 2
    sb_end = 3 * sb_w
    df_end = sb_end + 3 * df_w
    if col < sb_w:
        return "scale"
    if col < sb_end:
        return "plain"
    if col < sb_end + 2 * df_w:
        return "norm"
    if col < df_end:
        return "plain"
    if col < df_end + 2 * sb_w:
        return "norm"
    return "plain"


def _proj_kernel(h_ref, g_ref, w_ref, cg_ref, gm_ref, o_ref, *, d_model, col_chunk):
    x = h_ref[...]
    ms = jnp.mean(x * x, axis=-1, keepdims=True)
    xn = (x * lax.rsqrt(ms + EPS) * g_ref[...]).astype(BF16)
    n_out = w_ref.shape[1]
    gm = gm_ref[...]
    for c0 in range(0, n_out, col_chunk):
        acc = jnp.dot(xn, w_ref[:, c0:c0 + col_chunk], preferred_element_type=F32)
        for s0 in range(0, col_chunk, LANES):
            col = c0 + s0
            blk = acc[:, s0:s0 + LANES]
            kind = _col_kind(col, d_model)
            if kind == "norm":
                hi, lo = _split_bf16(blk * blk)
                msq = (jnp.dot(hi, gm, preferred_element_type=F32)
                       + jnp.dot(lo, gm, preferred_element_type=F32))
                blk = blk * lax.rsqrt(msq + EPS) * cg_ref[:, col:col + LANES]
            elif kind == "scale":
                blk = blk * Q_FOLD
            o_ref[:, col:col + LANES] = blk.astype(BF16)


def _proj_call(h, gain, w_bf, col_gain, group_mean):
    n_tok, d_model = h.shape
    n_out = w_bf.shape[1]
    tm = ROW_TILE
    return pl.pallas_call(
        functools.partial(_proj_kernel, d_model=d_model, col_chunk=512),
        grid=(n_tok // tm,),
        in_specs=[pl.BlockSpec((tm, d_model), lambda i: (i, 0)),
                  pl.BlockSpec((1, d_model), lambda i: (0, 0)),
                  pl.BlockSpec((d_model, n_out), lambda i: (0, 0)),
                  pl.BlockSpec((1, n_out), lambda i: (0, 0)),
                  pl.BlockSpec((LANES, LANES), lambda i: (0, 0))],
        out_specs=pl.BlockSpec((tm, n_out), lambda i: (i, 0)),
        out_shape=jax.ShapeDtypeStruct((n_tok, n_out), BF16),
        compiler_params=_cparams(1), name="norm_in_proj")(
            h, gain.reshape(1, d_model), w_bf, col_gain, group_mean)


def _stack_halves(q):
    lane = lax.broadcasted_iota(jnp.int32, q.shape, 1)
    zero = jnp.zeros_like(q)
    return jnp.concatenate([jnp.where(lane < HEAD_DIM, q, zero),
                            jnp.where(lane >= HEAD_DIM, q, zero)], axis=0)


def _scores(q2, k):
    return lax.dot_general(q2, k, (((1,), (1,)), ((), ())), preferred_element_type=F32)


def _lane_tile(x, width):
    reps = width // LANES
    return x if reps == 1 else jnp.concatenate([x] * reps, axis=1)


def _lane_fold(x, op):
    out = x[:, :LANES]
    for c in range(LANES, x.shape[1], LANES):
        out = op(out, x[:, c:c + LANES])
    return out


def _rows_tile(ref, j, tk):
    return ref[0, pl.ds(pl.multiple_of(j * tk, tk), tk), :]


def _for_each(n, fn, ways):
    def group(g, c):
        for w in range(ways):
            fn(g * ways + w)
        return c

    lax.fori_loop(0, n // ways, group, 0)
    part = ways // 2
    while part >= 1:
        start = n // (2 * part) * (2 * part)

        def tail(start=start, part=part):
            for w in range(part):
                fn(start + w)

        pl.when(n % (2 * part) >= part)(tail)
        part //= 2


def _merge_halves(y, tq):
    lane = lax.broadcasted_iota(jnp.int32, (tq, LANES), 1)
    return jnp.where(lane < HEAD_DIM, y[:tq], y[tq:])


SOFTPLUS_CLAMP = 126.0
F32_EXP2_UNDERFLOW = 152.0


def _sb_kernel(q_ref, k_ref, v_ref, u_ref, o_ref, acc_ref, skipped_ref, *, tq):
    i = pl.program_id(2)
    q2 = _stack_halves(q_ref[0])
    rows = 2 * tq
    acc_ref[...] = jnp.zeros(acc_ref.shape, F32)
    skipped_ref[...] = jnp.zeros(skipped_ref.shape, F32)

    def tile(j, diagonal):
        z = _scores(q2, _rows_tile(k_ref, j, tq))
        sp = jnp.maximum(jnp.log(1.0 + jnp.exp2(jnp.minimum(z, SOFTPLUS_CLAMP))) * LOG2E, z)
        if diagonal:
            row = lax.broadcasted_iota(jnp.int32, (rows, tq), 0)
            col = lax.broadcasted_iota(jnp.int32, (rows, tq), 1)
            earlier = col < jnp.where(row >= tq, row - tq, row)
            sp = jnp.where(earlier, sp, 0.0)
        hi, lo = _split_bf16(sp)
        suffix = jnp.dot(jnp.concatenate([hi, lo], axis=1), u_ref[...],
                         preferred_element_type=F32)
        skipped = skipped_ref[...]
        w = jnp.exp2(z - suffix - _lane_tile(skipped, tq))
        if diagonal:
            w = jnp.where(earlier, w, 0.0)
        acc_ref[...] += jnp.dot(w.astype(BF16), _rows_tile(v_ref, j, tq),
                                preferred_element_type=F32)
        skipped = skipped + jnp.broadcast_to(suffix[:, :1], skipped.shape)
        skipped_ref[...] = skipped
        return jnp.min(skipped)

    def more(state):
        t, least_skipped = state
        return (t <= i) & (least_skipped < F32_EXP2_UNDERFLOW)

    def step(state):
        t, _ = state
        return t + 1, tile(i - t, False)

    lax.while_loop(more, step, (jnp.int32(1), tile(i, True)))
    o_ref[0] = _merge_halves(acc_ref[...], tq).astype(o_ref.dtype)


BIAS_FAR, LOGIT_MAX, LOGIT_SPREAD = 0, 1, 2
SAFE_EXP2_RANGE = 80.0


def _qk_norm_reach(qk_norm):
    g = jnp.abs(qk_norm.astype(F32))
    return 1.01 * HEAD_DIM * Q_FOLD * jnp.max(g[0]) * jnp.max(g[1])


def _logit_stats(far, bias_max, bias_min, reach):
    return jnp.stack([far, bias_max + reach, bias_max - bias_min + 2.0 * reach])


def _diff_kernel(stats_ref, q_ref, k_ref, v_ref, near_ref, lam_ref, subln_ref, o_ref,
                 s_ref, p_ref, m_ref, l_ref, acc_ref, *, tq, lam_init):
    h = pl.program_id(1)
    i = pl.program_id(2)
    q2 = _stack_halves(q_ref[0])
    rows = 2 * tq
    far = stats_ref[BIAS_FAR, h]
    bounded = stats_ref[LOGIT_SPREAD, h] <= SAFE_EXP2_RANGE
    lmb = lam_ref[...]
    lam = (jnp.exp(jnp.sum(lmb[0:1] * lmb[1:2], axis=-1, keepdims=True))
           - jnp.exp(jnp.sum(lmb[2:3] * lmb[3:4], axis=-1, keepdims=True)) + lam_init)
    l_ref[...] = jnp.zeros(l_ref.shape, F32)
    acc_ref[...] = jnp.zeros(acc_ref.shape, F32)

    def biased_scores(j, bias):
        s = _scores(q2, _rows_tile(k_ref, j, tq))
        if bias is None:
            return s
        return (s.reshape(2, tq, tq) + bias[None]).reshape(rows, tq)

    @pl.when(bounded)
    def _():
        shift = stats_ref[LOGIT_MAX, h]

        def probs(j, bias, shift):
            p = jnp.exp2(biased_scores(j, bias) - shift)
            l_ref[...] += _lane_fold(p, jnp.add)
            p_ref[j] = p.astype(BF16)

        probs(i, near_ref[0, 0], shift)
        pl.when(i >= 1)(lambda: probs(i - 1, near_ref[0, 1], shift))
        _for_each(jnp.maximum(i - 1, 0), lambda j: probs(j, None, shift - far), 4)

        row_sum = jnp.sum(l_ref[...], axis=1, keepdims=True)
        mix = jnp.broadcast_to(-lam * row_sum[:tq] / row_sum[tq:], (tq, LANES)).astype(BF16)
        mix = _lane_tile(mix, tq)

        def weigh(j):
            p = p_ref[j]
            acc_ref[0:tq] += jnp.dot(p[:tq] + p[tq:] * mix, _rows_tile(v_ref, j, tq),
                                     preferred_element_type=F32)

        _for_each(i + 1, weigh, 4)
        acc_ref[0:tq] = acc_ref[0:tq] / row_sum[:tq]

    @pl.when(jnp.logical_not(bounded))
    def _():
        m_ref[...] = jnp.full(m_ref.shape, NEG, F32)

        def score(j, bias):
            s = biased_scores(j, bias)
            if bias is None:
                s = s + far
            s_ref[j] = s
            m_ref[...] = jnp.maximum(m_ref[...], _lane_fold(s, jnp.maximum))

        score(i, near_ref[0, 0])
        pl.when(i >= 1)(lambda: score(i - 1, near_ref[0, 1]))
        _for_each(jnp.maximum(i - 1, 0), lambda j: score(j, None), 4)
        m_ref[...] = jnp.broadcast_to(jnp.max(m_ref[...], axis=1, keepdims=True), m_ref.shape)

        def weigh(j):
            p = jnp.exp2(s_ref[j] - _lane_tile(m_ref[...], tq))
            l_ref[...] += _lane_fold(p, jnp.add)
            acc_ref[...] += jnp.dot(p.astype(BF16), _rows_tile(v_ref, j, tq),
                                    preferred_element_type=F32)

        _for_each(i + 1, weigh, 4)
        y = acc_ref[...] / jnp.sum(l_ref[...], axis=1, keepdims=True)
        acc_ref[0:tq] = y[:tq] - lam * y[tq:]

    y = acc_ref[0:tq]
    ms = jnp.mean(y * y, axis=-1, keepdims=True)
    y = y * lax.rsqrt(ms + EPS) * subln_ref[...] * (1.0 - lam_init)
    o_ref[0] = y.astype(o_ref.dtype)


def _ca_kernel(stats_ref, q_ref, k_ref, v_ref, bias_ref, o_ref, *, tq, n_near):
    h = pl.program_id(1)
    i = pl.program_id(2)
    q2 = _stack_halves(q_ref[0])
    bounded = stats_ref[LOGIT_SPREAD, h] <= SAFE_EXP2_RANGE

    def scores(d):
        j = jnp.maximum(i - d, 0)
        s = _scores(q2, _rows_tile(k_ref, j, tq)) + bias_ref[0, d]
        if d > 0:
            s = s + jnp.where(i >= d, 0.0, NEG)
        return j, s

    def attend(tiles, shift):
        l = None
        acc = None
        for j, s in tiles:
            p = jnp.exp2(s - shift)
            pv = jnp.dot(p.astype(BF16), _rows_tile(v_ref, j, tq), preferred_element_type=F32)
            psum = _lane_fold(p, jnp.add)
            l = psum if l is None else l + psum
            acc = pv if acc is None else acc + pv
        y = acc / jnp.sum(l, axis=1, keepdims=True)
        o_ref[0] = _merge_halves(y, tq).astype(o_ref.dtype)

    @pl.when(bounded)
    def _():
        attend((scores(d) for d in range(n_near)), stats_ref[LOGIT_MAX, h])

    @pl.when(jnp.logical_not(bounded))
    def _():
        tiles = [scores(d) for d in range(n_near)]
        m = _lane_fold(tiles[0][1], jnp.maximum)
        for _, s in tiles[1:]:
            m = jnp.maximum(m, _lane_fold(s, jnp.maximum))
        m = jnp.broadcast_to(jnp.max(m, axis=1, keepdims=True), m.shape)
        attend(tiles, _lane_tile(m, tq))


def _t5_bucket(rel):
    nb = T5_BUCKETS // 2
    max_exact = nb // 2
    ret = jnp.where(rel > 0, nb, 0)
    n = jnp.abs(rel)
    large = max_exact + (jnp.log(jnp.maximum(n, 1).astype(F32) / max_exact)
                         / math.log(T5_MAX_DIST / max_exact) * (nb - max_exact)).astype(jnp.int32)
    large = jnp.minimum(large, nb - 1)
    return ret + jnp.where(n < max_exact, n, large)


def _toeplitz_tile(bias_of_rel, tq, d):
    span = 2 * tq
    x = jnp.arange(span, dtype=jnp.int32)
    x = jnp.where(x < tq, x, x - span)
    vec = jnp.transpose(bias_of_rel(x - d * tq)).astype(F32) * LOG2E
    flat = jnp.tile(vec, (1, tq))[:, :tq * (span - 1)]
    return flat.reshape(vec.shape[0], tq, span - 1)[:, :, :tq]


def _diff_bias_tables(t5_bias, qk_norm, tq):
    assert tq >= T5_MAX_DIST and tq % CHUNK == 0
    r = jnp.arange(tq)[:, None]
    c = jnp.arange(tq)[None, :]
    bias_of_rel = lambda rel: t5_bias[_t5_bucket(rel)]
    diag = jnp.where((c // CHUNK) <= (r // CHUNK), _toeplitz_tile(bias_of_rel, tq, 0), NEG)
    scaled = t5_bias.astype(F32) * LOG2E
    far = scaled[_t5_bucket(jnp.int32(-2 * tq))]
    stats = _logit_stats(far, jnp.max(scaled, axis=0), jnp.min(scaled, axis=0),
                         _qk_norm_reach(qk_norm))
    return jnp.stack([diag, _toeplitz_tile(bias_of_rel, tq, 1)], axis=1), stats


def _ca_bias_tables(rel_bias, qk_norm, tq):
    assert (CA_LEFT_CHUNKS * CHUNK) % tq == 0 and tq % CHUNK == 0
    n_near = CA_LEFT_CHUNKS * CHUNK // tq + 1
    r = jnp.arange(tq)[:, None]
    c = jnp.arange(tq)[None, :]
    bias_of_rel = lambda rel: rel_bias[jnp.clip(rel, -CA_REL_CLIP, CA_REL_CLIP) + CA_REL_CLIP]
    tiles = []
    for d in range(n_near):
        gap = d * (tq // CHUNK) + r // CHUNK - c // CHUNK
        tiles.append(jnp.where((gap >= 0) & (gap <= CA_LEFT_CHUNKS),
                               _toeplitz_tile(bias_of_rel, tq, d), NEG))
    t = jnp.stack(tiles, axis=1)
    n_heads = t.shape[0]
    t = t.reshape(n_heads // 2, 2, n_near, tq, tq).transpose(0, 2, 1, 3, 4)
    scaled = (rel_bias.astype(F32) * LOG2E).reshape(rel_bias.shape[0], n_heads // 2, 2)
    top = jnp.max(scaled, axis=(0, 2))
    stats = _logit_stats(jnp.zeros_like(top), top, jnp.min(scaled, axis=(0, 2)),
                         _qk_norm_reach(qk_norm))
    return t.reshape(n_heads // 2, n_near, 2 * tq, tq), stats, n_near


def _attention_calls(proj, batch, t5_bias, diff_qk_norm, diff_lambda, diff_subln, ca_qk_norm,
                     ca_rel_bias, lam_init):
    n_tok, n_proj = proj.shape
    seq = n_tok // batch
    d_model = n_proj // 3
    sb_w = d_model // 4
    proj3 = proj.reshape(batch, seq, n_proj)
    tq = ATT_TILE
    nq = seq // tq
    sb_blocks = sb_w // LANES
    df_heads = (d_model // 2) // LANES
    sb_q0 = 0
    df_q0 = 3 * sb_blocks
    ca_q0 = df_q0 + 3 * df_heads

    def qkv_specs(q0, stride):
        return [pl.BlockSpec((1, tq, LANES), lambda b, h, i: (b, i, q0 + h)),
                pl.BlockSpec((1, seq, LANES), lambda b, h, i: (b, 0, q0 + stride + h)),
                pl.BlockSpec((1, seq, LANES), lambda b, h, i: (b, 0, q0 + 2 * stride + h))]

    out_spec = pl.BlockSpec((1, tq, LANES), lambda b, h, i: (b, i, h))
    stacked = pltpu.VMEM((2 * tq, LANES), F32)

    lower = (jnp.arange(tq)[:, None] >= jnp.arange(tq)[None, :]).astype(BF16)
    u = jnp.concatenate([lower, lower], axis=0)
    ya = pl.pallas_call(
        functools.partial(_sb_kernel, tq=tq),
        grid=(batch, sb_blocks, nq),
        in_specs=qkv_specs(sb_q0, sb_blocks) + [pl.BlockSpec(u.shape, lambda b, h, i: (0, 0))],
        out_specs=out_spec,
        out_shape=jax.ShapeDtypeStruct((batch, seq, sb_w), BF16),
        scratch_shapes=[stacked, stacked],
        compiler_params=_cparams(3), name="stick_breaking_attn")(proj3, proj3, proj3, u)

    near, bias_stats = _diff_bias_tables(t5_bias, diff_qk_norm, tq)
    yd = pl.pallas_call(
        functools.partial(_diff_kernel, tq=tq, lam_init=lam_init),
        grid=(batch, df_heads, nq),
        in_specs=[pl.BlockSpec(memory_space=pltpu.SMEM)] + qkv_specs(df_q0, df_heads) + [
            pl.BlockSpec((1, 2, tq, tq), lambda b, h, i: (h, 0, 0, 0)),
            pl.BlockSpec((4, HEAD_DIM), lambda b, h, i: (0, 0)),
            pl.BlockSpec((1, LANES), lambda b, h, i: (0, 0))],
        out_specs=out_spec,
        out_shape=jax.ShapeDtypeStruct((batch, seq, d_model // 2), BF16),
        scratch_shapes=[pltpu.VMEM((nq, 2 * tq, tq), F32), pltpu.VMEM((nq, 2 * tq, tq), BF16),
                        stacked, stacked, stacked],
        compiler_params=_cparams(3), name="differential_attn")(
            bias_stats, proj3, proj3, proj3, near, diff_lambda.astype(F32),
            diff_subln.astype(F32).reshape(1, LANES))

    ca_bias, ca_stats, n_near = _ca_bias_tables(ca_rel_bias, ca_qk_norm, tq)
    yc = pl.pallas_call(
        functools.partial(_ca_kernel, tq=tq, n_near=n_near),
        grid=(batch, sb_blocks, nq),
        in_specs=[pl.BlockSpec(memory_space=pltpu.SMEM)] + qkv_specs(ca_q0, sb_blocks) + [
            pl.BlockSpec((1, n_near, 2 * tq, tq), lambda b, h, i: (h, 0, 0, 0))],
        out_specs=out_spec,
        out_shape=jax.ShapeDtypeStruct((batch, seq, sb_w), BF16),
        compiler_params=_cparams(3), name="chunked_rel_attn")(
            ca_stats, proj3, proj3, proj3, ca_bias)

    return (ya.reshape(n_tok, sb_w), yd.reshape(n_tok, d_model // 2), yc.reshape(n_tok, sb_w))


META_E, META_GATE, META_RANK = 0, 2, 4
META_ROWS = 8


def _row_min_lane(mask, lane_f):
    return jnp.min(jnp.where(mask, lane_f, float(LANES)), axis=1, keepdims=True)


def _out_proj_kernel(h_ref, ya_ref, yd_ref, yc_ref, wo_ref, g_ref, rhi_ref, rlo_ref, rb_ref,
                     ltri_ref, h2_ref, f_ref, meta_ref, meta_t_ref, count_ref, run_ref):
    wa = ya_ref.shape[1]
    wd = yd_ref.shape[1]
    h2 = (h_ref[...]
          + jnp.dot(ya_ref[...], wo_ref[0:wa, :], preferred_element_type=F32)
          + jnp.dot(yd_ref[...], wo_ref[wa:wa + wd, :], preferred_element_type=F32)
          + jnp.dot(yc_ref[...], wo_ref[wa + wd:, :], preferred_element_type=F32))
    h2_ref[...] = h2
    ms = jnp.mean(h2 * h2, axis=-1, keepdims=True)
    f = h2 * lax.rsqrt(ms + EPS) * g_ref[...]
    f_ref[...] = f
    f_hi, f_lo = _split_bf16(f)
    logits = (jnp.dot(f_hi, rhi_ref[...], preferred_element_type=F32)
              + jnp.dot(f_lo, rhi_ref[...], preferred_element_type=F32)
              + jnp.dot(f_hi, rlo_ref[...], preferred_element_type=F32)
              + rb_ref[...])

    lane = lax.broadcasted_iota(jnp.int32, logits.shape, 1)
    lane_f = lane.astype(F32)
    is_group = lane < N_GROUPS
    gl = jnp.where(is_group, logits, NEG)
    g_max = jnp.max(gl, axis=1, keepdims=True)
    g_sel = _row_min_lane(gl == g_max, lane_f)
    g_w = 1.0 / jnp.sum(jnp.where(is_group, jnp.exp(logits - g_max), 0.0), axis=1, keepdims=True)
    first_lane = float(N_GROUPS) + float(EXPERTS_PER_GROUP) * g_sel
    in_group = (lane_f >= first_lane) & (lane_f < first_lane + float(EXPERTS_PER_GROUP))
    el = jnp.where(in_group, logits, NEG)
    m1 = jnp.max(el, axis=1, keepdims=True)
    i1 = _row_min_lane(el == m1, lane_f)
    el = jnp.where(lane_f == i1, NEG, el)
    m2 = jnp.max(el, axis=1, keepdims=True)
    i2 = _row_min_lane(el == m2, lane_f)
    r = jnp.exp(m2 - m1)
    p1 = 1.0 / (1.0 + r)
    gate1 = g_w * p1
    gate2 = g_w * (r * p1)

    @pl.when(pl.program_id(0) == 0)
    def _():
        run_ref[...] = jnp.zeros(run_ref.shape, F32)

    hit1 = lane_f == i1
    hit2 = lane_f == i2
    one_hot = jnp.where(hit1 | hit2, 1.0, 0.0)
    pos = run_ref[...] + jnp.dot(ltri_ref[...], one_hot.astype(BF16), preferred_element_type=F32)
    rank1 = jnp.sum(jnp.where(hit1, pos, 0.0), axis=1, keepdims=True)
    rank2 = jnp.sum(jnp.where(hit2, pos, 0.0), axis=1, keepdims=True)
    run_ref[...] += jnp.sum(one_hot, axis=0, keepdims=True)
    count_ref[...] = run_ref[...]

    meta = jnp.zeros(logits.shape, F32)
    for at, val in ((META_E, i1 - N_GROUPS), (META_E + 1, i2 - N_GROUPS), (META_GATE, gate1),
                    (META_GATE + 1, gate2), (META_RANK, rank1), (META_RANK + 1, rank2)):
        meta = jnp.where(lane == at, val, meta)
    meta_ref[...] = meta
    meta_t_ref[...] = jnp.transpose(meta)[:META_ROWS]


def _out_proj_call(h, ya, yd, yc, wo_bf, gain, r_hi, r_lo, r_bias):
    n_tok, d_model = h.shape
    tm = ROW_TILE
    row = lambda w: pl.BlockSpec((tm, w), lambda i: (i, 0))
    full = lambda a: pl.BlockSpec(a.shape, lambda i: (0, 0))
    gain = gain.reshape(1, d_model)
    ltri = (jnp.arange(tm)[:, None] > jnp.arange(tm)[None, :]).astype(BF16)
    return pl.pallas_call(
        _out_proj_kernel,
        grid=(n_tok // tm,),
        in_specs=[row(d_model), row(ya.shape[1]), row(yd.shape[1]), row(yc.shape[1]),
                  full(wo_bf), full(gain), full(r_hi), full(r_lo), full(r_bias), full(ltri)],
        out_specs=[row(d_model), row(d_model), row(LANES),
                   pl.BlockSpec((META_ROWS, tm), lambda i: (0, i)),
                   pl.BlockSpec((1, LANES), lambda i: (0, 0))],
        out_shape=[jax.ShapeDtypeStruct((n_tok, d_model), F32),
                   jax.ShapeDtypeStruct((n_tok, d_model), F32),
                   jax.ShapeDtypeStruct((n_tok, LANES), F32),
                   jax.ShapeDtypeStruct((META_ROWS, n_tok), F32),
                   jax.ShapeDtypeStruct((1, LANES), F32)],
        scratch_shapes=[pltpu.VMEM((1, LANES), F32)],
        compiler_params=_cparams(1), name="out_proj_router")(
            h, ya, yd, yc, wo_bf, gain, r_hi, r_lo, r_bias, ltri)


def _dispatch_plan(meta_t, count_row, tb, tm):
    n_tok = meta_t.shape[1]
    counts = count_row[0, N_GROUPS:N_GROUPS + N_EXPERTS].astype(jnp.int32)
    padded = (counts + tb - 1) // tb * tb
    e = jnp.arange(N_EXPERTS)
    pend = jnp.sum(jnp.where(e[:, None] >= e[None, :], padded[None, :], 0), axis=1)
    pstart = pend - padded
    n_rows = n_tok * TOP_K_INNER + N_EXPERTS * tb
    block_start = jnp.arange(n_rows // tb, dtype=jnp.int32) * tb
    blk_expert = jnp.minimum(jnp.sum((pend[None, :] <= block_start[:, None]).astype(jnp.int32), axis=1),
                             N_EXPERTS - 1)
    n_used = pend[-1:] // tb
    expert = meta_t[META_E:META_E + TOP_K_INNER].astype(jnp.int32)
    rank = meta_t[META_RANK:META_RANK + TOP_K_INNER].astype(jnp.int32)
    seg_start = jnp.sum(jnp.where(expert[..., None] == e, pstart, 0), axis=-1)
    idx = jnp.transpose(seg_start + rank).reshape(n_tok // tm, 1, tm * TOP_K_INNER)
    return pend.astype(jnp.int32), blk_expert, n_used, idx, n_rows


def _for_each_assignment(tm, fn):
    def body(r, c):
        for k in range(TOP_K_INNER):
            fn(r, k)
        return c

    lax.fori_loop(0, tm, body, 0, unroll=8)


def _dispatch_kernel(pend_ref, idx_ref, f_ref, xs_ref, zero_ref, row_sem, zero_sem, *, tb):
    tm = f_ref.shape[0]

    @pl.when(pl.program_id(0) == 0)
    def _():
        zero_ref[...] = jnp.zeros(zero_ref.shape, F32)

        def block_copy(start):
            return pltpu.make_async_copy(zero_ref, xs_ref.at[pl.ds(pl.multiple_of(start, tb), tb)],
                                         zero_sem)

        for e in range(N_EXPERTS):
            block_copy(jnp.maximum(pend_ref[e] - tb, 0)).start()
        for e in range(N_EXPERTS):
            block_copy(jnp.maximum(pend_ref[e] - tb, 0)).wait()

        def fill(b, c):
            block_copy(b * tb).start()
            block_copy(b * tb).wait()
            return c

        first_unused = lax.shift_right_logical(pend_ref[N_EXPERTS - 1], tb.bit_length() - 1)
        lax.fori_loop(first_unused, xs_ref.shape[0] // tb, fill, 0)

    def row_copy(r, k):
        return pltpu.make_async_copy(f_ref.at[pl.ds(r, 1)], xs_ref.at[pl.ds(idx_ref[0, 0, r * TOP_K_INNER + k], 1)],
                                     row_sem)

    _for_each_assignment(tm, lambda r, k: row_copy(r, k).start(priority=k))
    _for_each_assignment(tm, lambda r, k: row_copy(r, k).wait())


def _dispatch_call(f, pend, idx, n_rows):
    n_tok, d_model = f.shape
    tm = idx.shape[2] // TOP_K_INNER
    tb = MOE_TILE
    assert tb & (tb - 1) == 0
    grid_spec = pltpu.PrefetchScalarGridSpec(
        num_scalar_prefetch=1,
        grid=(n_tok // tm,),
        in_specs=[pl.BlockSpec((1,) + idx.shape[1:], lambda i, pe: (i, 0, 0),
                               memory_space=pltpu.SMEM),
                  pl.BlockSpec((tm, d_model), lambda i, pe: (i, 0))],
        out_specs=pl.BlockSpec(memory_space=pl.ANY),
        scratch_shapes=[pltpu.VMEM((tb, d_model), F32), pltpu.SemaphoreType.DMA(()),
                        pltpu.SemaphoreType.DMA(())])
    return pl.pallas_call(
        functools.partial(_dispatch_kernel, tb=tb), grid_spec=grid_spec,
        out_shape=jax.ShapeDtypeStruct((n_rows, d_model), F32),
        compiler_params=_cparams(1), name="moe_dispatch")(pend, idx, f)


def _expert_kernel(blk_expert_ref, n_used_ref, x_ref, wg_ref, wu_ref, wd_ref, y_ref,
                   wg_bf, wu_bf, wd_bf):
    i = pl.program_id(0)
    used = i < n_used_ref[0]
    e = blk_expert_ref[i]
    e_prev = blk_expert_ref[jnp.maximum(i - 1, 0)]

    @pl.when(used & ((i == 0) | (e != e_prev)))
    def _():
        wg_bf[...] = wg_ref[0, 0].astype(BF16)
        wu_bf[...] = wu_ref[0, 0].astype(BF16)
        wd_bf[...] = wd_ref[0, 0].astype(BF16)

    @pl.when(used)
    def _():
        x = x_ref[...].astype(BF16)
        g = jnp.dot(x, wg_bf[...], preferred_element_type=F32)
        u = jnp.dot(x, wu_bf[...], preferred_element_type=F32)
        hid = (g / (1.0 + jnp.exp(-g)) * u).astype(BF16)
        y_ref[...] = jnp.dot(hid, wd_bf[...], preferred_element_type=F32)

    @pl.when(jnp.logical_not(used))
    def _():
        y_ref[...] = jnp.zeros(y_ref.shape, F32)


def _expert_call(xs, blk_expert, n_used, w_gate, w_up, w_down, layer):
    n_rows, d_model = xs.shape
    d_exp = w_gate.shape[3]
    tb = MOE_TILE
    blk = lambda i, be, nu: jnp.minimum(i, nu[0] - 1)
    weight = lambda shape: pl.BlockSpec(
        (1, 1) + shape, lambda i, be, nu: (layer, be[blk(i, be, nu)], 0, 0))
    grid_spec = pltpu.PrefetchScalarGridSpec(
        num_scalar_prefetch=2,
        grid=(n_rows // tb,),
        in_specs=[pl.BlockSpec((tb, d_model), lambda i, be, nu: (blk(i, be, nu), 0)),
                  weight((d_model, d_exp)), weight((d_model, d_exp)), weight((d_exp, d_model))],
        out_specs=pl.BlockSpec((tb, d_model), lambda i, be, nu: (i, 0)),
        scratch_shapes=[pltpu.VMEM((d_model, d_exp), BF16), pltpu.VMEM((d_model, d_exp), BF16),
                        pltpu.VMEM((d_exp, d_model), BF16)])
    return pl.pallas_call(
        _expert_kernel, grid_spec=grid_spec,
        out_shape=jax.ShapeDtypeStruct(xs.shape, F32),
        compiler_params=_cparams(1), name="expert_mlp")(
            blk_expert, n_used, xs, w_gate, w_up, w_down)


def _combine_kernel(idx_ref, h_ref, meta_ref, y_ref, o_ref, ybuf_ref, sem):
    tm = h_ref.shape[0]

    def row_copy(r, k):
        return pltpu.make_async_copy(y_ref.at[pl.ds(idx_ref[0, 0, r * TOP_K_INNER + k], 1)],
                                     ybuf_ref.at[k, pl.ds(r, 1)], sem)

    _for_each_assignment(tm, lambda r, k: row_copy(r, k).start(priority=k))
    _for_each_assignment(tm, lambda r, k: row_copy(r, k).wait())
    meta = meta_ref[...]
    out = h_ref[...]
    for k in range(TOP_K_INNER):
        out = out + meta[:, META_GATE + k:META_GATE + k + 1] * ybuf_ref[k]
    o_ref[...] = out


def _combine_call(h2, meta, y_rows, idx):
    n_tok, d_model = h2.shape
    tm = idx.shape[2] // TOP_K_INNER
    return pl.pallas_call(
        _combine_kernel,
        grid=(n_tok // tm,),
        in_specs=[pl.BlockSpec((1,) + idx.shape[1:], lambda i: (i, 0, 0), memory_space=pltpu.SMEM),
                  pl.BlockSpec((tm, d_model), lambda i: (i, 0)),
                  pl.BlockSpec((tm, LANES), lambda i: (i, 0)),
                  pl.BlockSpec(memory_space=pl.ANY)],
        out_specs=pl.BlockSpec((tm, d_model), lambda i: (i, 0)),
        out_shape=jax.ShapeDtypeStruct((n_tok, d_model), F32),
        scratch_shapes=[pltpu.VMEM((TOP_K_INNER, tm, d_model), F32), pltpu.SemaphoreType.DMA(())],
        compiler_params=_cparams(1), name="moe_combine")(idx, h2, meta, y_rows)


def _moe(h2, f, meta, meta_t, count_row, w_gate, w_up, w_down, layer):
    pend, blk_expert, n_used, idx, n_rows = _dispatch_plan(meta_t, count_row, MOE_TILE, ROW_TILE)
    xs = _dispatch_call(f, pend, idx, n_rows)
    y_rows = _expert_call(xs, blk_expert, n_used, w_gate, w_up, w_down, layer)
    return _combine_call(h2, meta, y_rows, idx)


def _proj_col_gain(diff_qk_norm, ca_qk_norm, d_model):
    sb_w, df_w = d_model // 4, d_model // 2
    ones = lambda n: jnp.ones((n,), F32)
    rep = lambda g, n: jnp.tile(g.astype(F32), n // HEAD_DIM)
    return jnp.concatenate([
        ones(3 * sb_w),
        rep(diff_qk_norm[0], df_w) * Q_FOLD, rep(diff_qk_norm[1], df_w), ones(df_w),
        rep(ca_qk_norm[0], sb_w) * Q_FOLD, rep(ca_qk_norm[1], sb_w), ones(sb_w)]).reshape(1, -1)


def kernel(x, t5_bias, attn_norm, w_in, diff_qk_norm, diff_lambda, diff_subln, ca_qk_norm,
           ca_rel_bias, w_out, ffn_norm, router_group_w, router_group_b, router_expert_w,
           router_expert_b, expert_w_gate, expert_w_up, expert_w_down):
    batch, seq, d_model = x.shape
    depth = w_in.shape[0]
    n_tok = batch * seq
    lane_group = jnp.arange(LANES) // HEAD_DIM
    group_mean = ((lane_group[:, None] == lane_group[None, :]).astype(F32) / HEAD_DIM).astype(BF16)

    h = x.reshape(n_tok, d_model)
    for l in range(depth):
        lam_init = 0.8 - 0.6 * math.exp(-0.3 * l)
        col_gain = _proj_col_gain(diff_qk_norm[l], ca_qk_norm[l], d_model)
        proj = _proj_call(h, attn_norm[l], w_in[l].astype(BF16), col_gain, group_mean)
        ya, yd, yc = _attention_calls(proj, batch, t5_bias, diff_qk_norm[l], diff_lambda[l],
                                      diff_subln[l], ca_qk_norm[l], ca_rel_bias[l], lam_init)
        r_w = jnp.concatenate([router_group_w[l], router_expert_w[l]], axis=1).astype(F32)
        r_w = jnp.pad(r_w, ((0, 0), (0, LANES - r_w.shape[1])))
        r_hi, r_lo = _split_bf16(r_w)
        r_b = jnp.concatenate([router_group_b[l], router_expert_b[l]]).astype(F32)
        r_b = jnp.pad(r_b, (0, LANES - r_b.shape[0])).reshape(1, LANES)
        h2, f, meta, meta_t, count_row = _out_proj_call(h, ya, yd, yc, w_out[l].astype(BF16),
                                                        ffn_norm[l], r_hi, r_lo, r_b)
        h = _moe(h2, f, meta, meta_t, count_row, expert_w_gate, expert_w_up, expert_w_down, l)
    return h.reshape(batch, seq, d_model)
```

```python
import functools
import math

import jax
import jax.numpy as jnp
from jax import lax
from jax.experimental import pallas as pl
from jax.experimental.pallas import tpu as pltpu

F32 = jnp.float32
BF16 = jnp.bfloat16

LANES = 128
HEAD_DIM = 64
CHUNK = 64
CA_LEFT_CHUNKS = 8
CA_REL_CLIP = 128
T5_BUCKETS = 32
T5_MAX_DIST = 256
N_GROUPS = 4
EXPERTS_PER_GROUP = 8
N_EXPERTS = N_GROUPS * EXPERTS_PER_GROUP
TOP_K_INNER = 2
EPS = 1e-6
NEG = -1e30
LOG2E = math.log2(math.e)
Q_FOLD = HEAD_DIM ** -0.5 * LOG2E

ROW_TILE = 512
ATT_TILE = 256
MOE_TILE = 512
VMEM_LIMIT = 48 * 1024 * 1024


def _cparams(n_axes):
    return pltpu.CompilerParams(dimension_semantics=("arbitrary",) * n_axes,
                                vmem_limit_bytes=VMEM_LIMIT)


def _split_bf16(x):
    hi = x.astype(BF16)
    lo = (x - hi.astype(F32)).astype(BF16)
    return hi, lo


def _col_kind(col, d_model):
    sb_w, df_w = d_model // 4, d_model // 2
    sb_end = 3 * sb_w
    df_end = sb_end + 3 * df_w
    if col < sb_w:
        return "scale"
    if col < sb_end:
        return "plain"
    if col < sb_end + 2 * df_w:
        return "norm"
    if col < df_end:
        return "plain"
    if col < df_end + 2 * sb_w:
        return "norm"
    return "plain"


def _proj_kernel(h_ref, g_ref, w_ref, cg_ref, gm_ref, o_ref, *, d_model, col_chunk):
    x = h_ref[...]
    ms = jnp.mean(x * x, axis=-1, keepdims=True)
    xn = (x * lax.rsqrt(ms + EPS) * g_ref[...]).astype(BF16)
    n_out = w_ref.shape[1]
    gm = gm_ref[...]
    for c0 in range(0, n_out, col_chunk):
        acc = jnp.dot(xn, w_ref[:, c0:c0 + col_chunk], preferred_element_type=F32)
        for s0 in range(0, col_chunk, LANES):
            col = c0 + s0
            blk = acc[:, s0:s0 + LANES]
            kind = _col_kind(col, d_model)
            if kind == "norm":
                hi, lo = _split_bf16(blk * blk)
                msq = (jnp.dot(hi, gm, preferred_element_type=F32)
                       + jnp.dot(lo, gm, preferred_element_type=F32))
                blk = blk * lax.rsqrt(msq + EPS) * cg_ref[:, col:col + LANES]
            elif kind == "scale":
                blk = blk * Q_FOLD
            o_ref[:, col:col + LANES] = blk.astype(BF16)


def _proj_call(h, gain, w_bf, col_gain, group_mean):
    n_tok, d_model = h.shape
    n_out = w_bf.shape[1]
    tm = ROW_TILE
    return pl.pallas_call(
        functools.partial(_proj_kernel, d_model=d_model, col_chunk=512),
        grid=(n_tok // tm,),
        in_specs=[pl.BlockSpec((tm, d_model), lambda i: (i, 0)),
                  pl.BlockSpec((1, d_model), lambda i: (0, 0)),
                  pl.BlockSpec((d_model, n_out), lambda i: (0, 0)),
                  pl.BlockSpec((1, n_out), lambda i: (0, 0)),
                  pl.BlockSpec((LANES, LANES), lambda i: (0, 0))],
        out_specs=pl.BlockSpec((tm, n_out), lambda i: (i, 0)),
        out_shape=jax.ShapeDtypeStruct((n_tok, n_out), BF16),
        compiler_params=_cparams(1), name="norm_in_proj")(
            h, gain.reshape(1, d_model), w_bf, col_gain, group_mean)


def _stack_halves(q):
    lane = lax.broadcasted_iota(jnp.int32, q.shape, 1)
    zero = jnp.zeros_like(q)
    return jnp.concatenate([jnp.where(lane < HEAD_DIM, q, zero),
                            jnp.where(lane >= HEAD_DIM, q, zero)], axis=0)


def _scores(q2, k):
    return lax.dot_general(q2, k, (((1,), (1,)), ((), ())), preferred_element_type=F32)


def _lane_tile(x, width):
    reps = width // LANES
    return x if reps == 1 else jnp.concatenate([x] * reps, axis=1)


def _lane_fold(x, op):
    out = x[:, :LANES]
    for c in range(LANES, x.shape[1], LANES):
        out = op(out, x[:, c:c + LANES])
    return out


def _rows_tile(ref, j, tk):
    return ref[0, pl.ds(pl.multiple_of(j * tk, tk), tk), :]


def _for_each(n, fn, ways):
    def group(g, c):
        for w in range(ways):
            fn(g * ways + w)
        return c

    lax.fori_loop(0, n // ways, group, 0)
    part = ways // 2
    while part >= 1:
        start = n // (2 * part) * (2 * part)

        def tail(start=start, part=part):
            for w in range(part):
                fn(start + w)

        pl.when(n % (2 * part) >= part)(tail)
        part //= 2


def _merge_halves(y, tq):
    lane = lax.broadcasted_iota(jnp.int32, (tq, LANES), 1)
    return jnp.where(lane < HEAD_DIM, y[:tq], y[tq:])


SOFTPLUS_CLAMP = 126.0
F32_EXP2_UNDERFLOW = 152.0


def _sb_attend(i, q_ref, k_ref, v_ref, u_ref, o_ref, acc_ref, skipped_ref, *, tq):
    q2 = _stack_halves(q_ref[0])
    rows = 2 * tq
    acc_ref[...] = jnp.zeros(acc_ref.shape, F32)
    skipped_ref[...] = jnp.zeros(skipped_ref.shape, F32)

    def tile(j, diagonal):
        z = _scores(q2, _rows_tile(k_ref, j, tq))
        sp = jnp.maximum(jnp.log(1.0 + jnp.exp2(jnp.minimum(z, SOFTPLUS_CLAMP))) * LOG2E, z)
        if diagonal:
            row = lax.broadcasted_iota(jnp.int32, (rows, tq), 0)
            col = lax.broadcasted_iota(jnp.int32, (rows, tq), 1)
            earlier = col < jnp.where(row >= tq, row - tq, row)
            sp = jnp.where(earlier, sp, 0.0)
        hi, lo = _split_bf16(sp)
        suffix = jnp.dot(jnp.concatenate([hi, lo], axis=1), u_ref[...],
                         preferred_element_type=F32)
        skipped = skipped_ref[...]
        w = jnp.exp2(z - suffix - _lane_tile(skipped, tq))
        if diagonal:
            w = jnp.where(earlier, w, 0.0)
        acc_ref[...] += jnp.dot(w.astype(BF16), _rows_tile(v_ref, j, tq),
                                preferred_element_type=F32)
        skipped = skipped + jnp.broadcast_to(suffix[:, :1], skipped.shape)
        skipped_ref[...] = skipped
        return jnp.min(skipped)

    def more(state):
        t, least_skipped = state
        return (t <= i) & (least_skipped < F32_EXP2_UNDERFLOW)

    def step(state):
        t, _ = state
        return t + 1, tile(i - t, False)

    lax.while_loop(more, step, (jnp.int32(1), tile(i, True)))
    o_ref[0] = _merge_halves(acc_ref[...], tq).astype(o_ref.dtype)


BIAS_FAR, LOGIT_MAX, LOGIT_SPREAD = 0, 1, 2
SAFE_EXP2_RANGE = 80.0


def _qk_norm_reach(qk_norm):
    g = jnp.abs(qk_norm.astype(F32))
    return 1.01 * HEAD_DIM * Q_FOLD * jnp.max(g[0]) * jnp.max(g[1])


def _logit_stats(far, bias_max, bias_min, reach):
    return jnp.stack([far, bias_max + reach, bias_max - bias_min + 2.0 * reach])


def _diff_attend(h, i, stats_ref, q_ref, k_ref, v_ref, near_ref, lam_ref, subln_ref, o_ref,
                 s_ref, p_ref, m_ref, l_ref, acc_ref, *, tq, lam_init):
    q2 = _stack_halves(q_ref[0])
    rows = 2 * tq
    far = stats_ref[BIAS_FAR, h]
    bounded = stats_ref[LOGIT_SPREAD, h] <= SAFE_EXP2_RANGE
    lmb = lam_ref[...]
    lam = (jnp.exp(jnp.sum(lmb[0:1] * lmb[1:2], axis=-1, keepdims=True))
           - jnp.exp(jnp.sum(lmb[2:3] * lmb[3:4], axis=-1, keepdims=True)) + lam_init)
    l_ref[...] = jnp.zeros(l_ref.shape, F32)
    acc_ref[...] = jnp.zeros(acc_ref.shape, F32)

    def biased_scores(j, bias):
        s = _scores(q2, _rows_tile(k_ref, j, tq))
        if bias is None:
            return s
        return (s.reshape(2, tq, tq) + bias[None]).reshape(rows, tq)

    @pl.when(bounded)
    def _():
        shift = stats_ref[LOGIT_MAX, h]

        def probs(j, bias, shift):
            p = jnp.exp2(biased_scores(j, bias) - shift)
            l_ref[...] += _lane_fold(p, jnp.add)
            p_ref[j] = p.astype(BF16)

        probs(i, near_ref[0, 0], shift)
        pl.when(i >= 1)(lambda: probs(i - 1, near_ref[0, 1], shift))
        _for_each(jnp.maximum(i - 1, 0), lambda j: probs(j, None, shift - far), 4)

        inv_l = 1.0 / jnp.sum(l_ref[...], axis=1, keepdims=True)
        m_ref[0:tq] = jnp.broadcast_to(inv_l[:tq], (tq, LANES))
        m_ref[tq:rows] = jnp.broadcast_to(-lam * inv_l[tq:], (tq, LANES))

        def weigh(j):
            p = p_ref[j]
            w = (p[:tq].astype(F32) * _lane_tile(m_ref[0:tq], tq)
                 + p[tq:].astype(F32) * _lane_tile(m_ref[tq:rows], tq))
            acc_ref[0:tq] += jnp.dot(w.astype(BF16), _rows_tile(v_ref, j, tq),
                                     preferred_element_type=F32)

        _for_each(i + 1, weigh, 4)

    @pl.when(jnp.logical_not(bounded))
    def _():
        m_ref[...] = jnp.full(m_ref.shape, NEG, F32)

        def score(j, bias):
            s = biased_scores(j, bias)
            if bias is None:
                s = s + far
            s_ref[j] = s
            m_ref[...] = jnp.maximum(m_ref[...], _lane_fold(s, jnp.maximum))

        score(i, near_ref[0, 0])
        pl.when(i >= 1)(lambda: score(i - 1, near_ref[0, 1]))
        _for_each(jnp.maximum(i - 1, 0), lambda j: score(j, None), 4)
        m_ref[...] = jnp.broadcast_to(jnp.max(m_ref[...], axis=1, keepdims=True), m_ref.shape)

        def weigh(j):
            p = jnp.exp2(s_ref[j] - _lane_tile(m_ref[...], tq))
            l_ref[...] += _lane_fold(p, jnp.add)
            acc_ref[...] += jnp.dot(p.astype(BF16), _rows_tile(v_ref, j, tq),
                                    preferred_element_type=F32)

        _for_each(i + 1, weigh, 4)
        y = acc_ref[...] / jnp.sum(l_ref[...], axis=1, keepdims=True)
        acc_ref[0:tq] = y[:tq] - lam * y[tq:]

    y = acc_ref[0:tq]
    ms = jnp.mean(y * y, axis=-1, keepdims=True)
    y = y * lax.rsqrt(ms + EPS) * subln_ref[...] * (1.0 - lam_init)
    o_ref[0] = y.astype(o_ref.dtype)


def _ca_attend(h, i, stats_ref, q_ref, k_ref, v_ref, bias_ref, o_ref, *, tq, n_near):
    q2 = _stack_halves(q_ref[0])
    bounded = stats_ref[LOGIT_SPREAD, h] <= SAFE_EXP2_RANGE

    def scores(d):
        j = jnp.maximum(i - d, 0)
        s = _scores(q2, _rows_tile(k_ref, j, tq)) + bias_ref[0, d]
        if d > 0:
            s = s + jnp.where(i >= d, 0.0, NEG)
        return j, s

    def attend(tiles, shift):
        l = None
        acc = None
        for j, s in tiles:
            p = jnp.exp2(s - shift)
            pv = jnp.dot(p.astype(BF16), _rows_tile(v_ref, j, tq), preferred_element_type=F32)
            psum = _lane_fold(p, jnp.add)
            l = psum if l is None else l + psum
            acc = pv if acc is None else acc + pv
        y = acc / jnp.sum(l, axis=1, keepdims=True)
        o_ref[0] = _merge_halves(y, tq).astype(o_ref.dtype)

    @pl.when(bounded)
    def _():
        attend((scores(d) for d in range(n_near)), stats_ref[LOGIT_MAX, h])

    @pl.when(jnp.logical_not(bounded))
    def _():
        tiles = [scores(d) for d in range(n_near)]
        m = _lane_fold(tiles[0][1], jnp.maximum)
        for _, s in tiles[1:]:
            m = jnp.maximum(m, _lane_fold(s, jnp.maximum))
        m = jnp.broadcast_to(jnp.max(m, axis=1, keepdims=True), m.shape)
        attend(tiles, _lane_tile(m, tq))


def _mixers_kernel(*refs, tq, lam_init, n_near, df_per_step):
    refs = list(refs)
    take = lambda n: [refs.pop(0) for _ in range(n)]
    diff_stats_ref, ca_stats_ref = take(2)
    sb_in = take(4)
    diff_in = [take(4) for _ in range(df_per_step)]
    lam_ref, subln_ref = take(2)
    ca_in = take(4)
    ya_ref, yd_ref, yc_ref = take(3)
    s_ref, p_ref, m_ref, l_ref, acc_ref, skipped_ref = refs
    g = pl.program_id(1)
    i = pl.program_id(2)
    _sb_attend(i, *sb_in, ya_ref, acc_ref, skipped_ref, tq=tq)
    for e, (q_ref, k_ref, v_ref, near_ref) in enumerate(diff_in):
        _diff_attend(df_per_step * g + e, i, diff_stats_ref, q_ref, k_ref, v_ref, near_ref,
                     lam_ref, subln_ref, yd_ref.at[:, :, pl.ds(e * LANES, LANES)],
                     s_ref, p_ref, m_ref, l_ref, acc_ref, tq=tq, lam_init=lam_init)
    _ca_attend(g, i, ca_stats_ref, *ca_in, yc_ref, tq=tq, n_near=n_near)


def _t5_bucket(rel):
    nb = T5_BUCKETS // 2
    max_exact = nb // 2
    ret = jnp.where(rel > 0, nb, 0)
    n = jnp.abs(rel)
    large = max_exact + (jnp.log(jnp.maximum(n, 1).astype(F32) / max_exact)
                         / math.log(T5_MAX_DIST / max_exact) * (nb - max_exact)).astype(jnp.int32)
    large = jnp.minimum(large, nb - 1)
    return ret + jnp.where(n < max_exact, n, large)


def _toeplitz_tile(bias_of_rel, tq, d):
    span = 2 * tq
    x = jnp.arange(span, dtype=jnp.int32)
    x = jnp.where(x < tq, x, x - span)
    vec = jnp.transpose(bias_of_rel(x - d * tq)).astype(F32) * LOG2E
    flat = jnp.tile(vec, (1, tq))[:, :tq * (span - 1)]
    return flat.reshape(vec.shape[0], tq, span - 1)[:, :, :tq]


def _diff_bias_tables(t5_bias, qk_norm, tq):
    assert tq >= T5_MAX_DIST and tq % CHUNK == 0
    r = jnp.arange(tq)[:, None]
    c = jnp.arange(tq)[None, :]
    bias_of_rel = lambda rel: t5_bias[_t5_bucket(rel)]
    diag = jnp.where((c // CHUNK) <= (r // CHUNK), _toeplitz_tile(bias_of_rel, tq, 0), NEG)
    scaled = t5_bias.astype(F32) * LOG2E
    far = scaled[_t5_bucket(jnp.int32(-2 * tq))]
    stats = _logit_stats(far, jnp.max(scaled, axis=0), jnp.min(scaled, axis=0),
                         _qk_norm_reach(qk_norm))
    return jnp.stack([diag, _toeplitz_tile(bias_of_rel, tq, 1)], axis=1), stats


def _ca_bias_tables(rel_bias, qk_norm, tq):
    assert (CA_LEFT_CHUNKS * CHUNK) % tq == 0 and tq % CHUNK == 0
    n_near = CA_LEFT_CHUNKS * CHUNK // tq + 1
    r = jnp.arange(tq)[:, None]
    c = jnp.arange(tq)[None, :]
    bias_of_rel = lambda rel: rel_bias[jnp.clip(rel, -CA_REL_CLIP, CA_REL_CLIP) + CA_REL_CLIP]
    tiles = []
    for d in range(n_near):
        gap = d * (tq // CHUNK) + r // CHUNK - c // CHUNK
        tiles.append(jnp.where((gap >= 0) & (gap <= CA_LEFT_CHUNKS),
                               _toeplitz_tile(bias_of_rel, tq, d), NEG))
    t = jnp.stack(tiles, axis=1)
    n_heads = t.shape[0]
    t = t.reshape(n_heads // 2, 2, n_near, tq, tq).transpose(0, 2, 1, 3, 4)
    scaled = (rel_bias.astype(F32) * LOG2E).reshape(rel_bias.shape[0], n_heads // 2, 2)
    top = jnp.max(scaled, axis=(0, 2))
    stats = _logit_stats(jnp.zeros_like(top), top, jnp.min(scaled, axis=(0, 2)),
                         _qk_norm_reach(qk_norm))
    return t.reshape(n_heads // 2, n_near, 2 * tq, tq), stats, n_near


def _attention_calls(proj, batch, t5_bias, diff_qk_norm, diff_lambda, diff_subln, ca_qk_norm,
                     ca_rel_bias, lam_init):
    n_tok, n_proj = proj.shape
    seq = n_tok // batch
    d_model = n_proj // 3
    sb_w = d_model // 4
    proj3 = proj.reshape(batch, seq, n_proj)
    tq = ATT_TILE
    nq = seq // tq
    sb_blocks = sb_w // LANES
    df_heads = (d_model // 2) // LANES
    sb_q0 = 0
    df_q0 = 3 * sb_blocks
    ca_q0 = df_q0 + 3 * df_heads

    assert df_heads % sb_blocks == 0
    df_per_step = df_heads // sb_blocks

    def qkv_specs(q0, stride, per_step=1, e=0):
        col = lambda g: q0 + per_step * g + e
        return [pl.BlockSpec((1, tq, LANES), lambda b, g, i: (b, i, col(g))),
                pl.BlockSpec((1, seq, LANES), lambda b, g, i: (b, 0, stride + col(g))),
                pl.BlockSpec((1, seq, LANES), lambda b, g, i: (b, 0, 2 * stride + col(g)))]

    smem = pl.BlockSpec(memory_space=pltpu.SMEM)
    out_spec = lambda w: pl.BlockSpec((1, tq, w), lambda b, g, i: (b, i, g))
    stacked = pltpu.VMEM((2 * tq, LANES), F32)

    lower = (jnp.arange(tq)[:, None] >= jnp.arange(tq)[None, :]).astype(BF16)
    u = jnp.concatenate([lower, lower], axis=0)
    near, bias_stats = _diff_bias_tables(t5_bias, diff_qk_norm, tq)
    ca_bias, ca_stats, n_near = _ca_bias_tables(ca_rel_bias, ca_qk_norm, tq)

    in_specs = [smem, smem] + qkv_specs(sb_q0, sb_blocks) + [
        pl.BlockSpec(u.shape, lambda b, g, i: (0, 0))]
    args = [bias_stats, ca_stats, proj3, proj3, proj3, u]
    for e in range(df_per_step):
        in_specs += qkv_specs(df_q0, df_heads, df_per_step, e) + [
            pl.BlockSpec((1, 2, tq, tq), lambda b, g, i, e=e: (df_per_step * g + e, 0, 0, 0))]
        args += [proj3, proj3, proj3, near]
    in_specs += [pl.BlockSpec((4, HEAD_DIM), lambda b, g, i: (0, 0)),
                 pl.BlockSpec((1, LANES), lambda b, g, i: (0, 0))]
    args += [diff_lambda.astype(F32), diff_subln.astype(F32).reshape(1, LANES)]
    in_specs += qkv_specs(ca_q0, sb_blocks) + [
        pl.BlockSpec((1, n_near, 2 * tq, tq), lambda b, g, i: (g, 0, 0, 0))]
    args += [proj3, proj3, proj3, ca_bias]

    ya, yd, yc = pl.pallas_call(
        functools.partial(_mixers_kernel, tq=tq, lam_init=lam_init, n_near=n_near,
                          df_per_step=df_per_step),
        grid=(batch, sb_blocks, nq),
        in_specs=in_specs,
        out_specs=[out_spec(LANES), out_spec(df_per_step * LANES), out_spec(LANES)],
        out_shape=[jax.ShapeDtypeStruct((batch, seq, sb_w), BF16),
                   jax.ShapeDtypeStruct((batch, seq, d_model // 2), BF16),
                   jax.ShapeDtypeStruct((batch, seq, sb_w), BF16)],
        scratch_shapes=[pltpu.VMEM((nq, 2 * tq, tq), F32), pltpu.VMEM((nq, 2 * tq, tq), BF16),
                        stacked, stacked, stacked, stacked],
        compiler_params=_cparams(3), name="attention_mixers")(*args)
    return (ya.reshape(n_tok, sb_w), yd.reshape(n_tok, d_model // 2), yc.reshape(n_tok, sb_w))


META_E, META_GATE, META_RANK = 0, 2, 4
META_ROWS = 8


def _row_min_lane(mask, lane_f):
    return jnp.min(jnp.where(mask, lane_f, float(LANES)), axis=1, keepdims=True)


def _out_proj_kernel(h_ref, ya_ref, yd_ref, yc_ref, wo_ref, g_ref, rhi_ref, rlo_ref, rb_ref,
                     ltri_ref, h2_ref, f_ref, meta_ref, meta_t_ref, count_ref, run_ref):
    wa = ya_ref.shape[1]
    wd = yd_ref.shape[1]
    h2 = (h_ref[...]
          + jnp.dot(ya_ref[...], wo_ref[0:wa, :], preferred_element_type=F32)
          + jnp.dot(yd_ref[...], wo_ref[wa:wa + wd, :], preferred_element_type=F32)
          + jnp.dot(yc_ref[...], wo_ref[wa + wd:, :], preferred_element_type=F32))
    h2_ref[...] = h2
    ms = jnp.mean(h2 * h2, axis=-1, keepdims=True)
    f = h2 * lax.rsqrt(ms + EPS) * g_ref[...]
    f_ref[...] = f
    f_hi, f_lo = _split_bf16(f)
    logits = (jnp.dot(f_hi, rhi_ref[...], preferred_element_type=F32)
              + jnp.dot(f_lo, rhi_ref[...], preferred_element_type=F32)
              + jnp.dot(f_hi, rlo_ref[...], preferred_element_type=F32)
              + rb_ref[...])

    lane = lax.broadcasted_iota(jnp.int32, logits.shape, 1)
    lane_f = lane.astype(F32)
    is_group = lane < N_GROUPS
    gl = jnp.where(is_group, logits, NEG)
    g_max = jnp.max(gl, axis=1, keepdims=True)
    g_sel = _row_min_lane(gl == g_max, lane_f)
    g_w = 1.0 / jnp.sum(jnp.where(is_group, jnp.exp(logits - g_max), 0.0), axis=1, keepdims=True)
    first_lane = float(N_GROUPS) + float(EXPERTS_PER_GROUP) * g_sel
    in_group = (lane_f >= first_lane) & (lane_f < first_lane + float(EXPERTS_PER_GROUP))
    el = jnp.where(in_group, logits, NEG)
    m1 = jnp.max(el, axis=1, keepdims=True)
    i1 = _row_min_lane(el == m1, lane_f)
    el = jnp.where(lane_f == i1, NEG, el)
    m2 = jnp.max(el, axis=1, keepdims=True)
    i2 = _row_min_lane(el == m2, lane_f)
    r = jnp.exp(m2 - m1)
    p1 = 1.0 / (1.0 + r)
    gate1 = g_w * p1
    gate2 = g_w * (r * p1)

    @pl.when(pl.program_id(0) == 0)
    def _():
        run_ref[...] = jnp.zeros(run_ref.shape, F32)

    hit1 = lane_f == i1
    hit2 = lane_f == i2
    one_hot = jnp.where(hit1 | hit2, 1.0, 0.0)
    pos = run_ref[...] + jnp.dot(ltri_ref[...], one_hot.astype(BF16), preferred_element_type=F32)
    rank1 = jnp.sum(jnp.where(hit1, pos, 0.0), axis=1, keepdims=True)
    rank2 = jnp.sum(jnp.where(hit2, pos, 0.0), axis=1, keepdims=True)
    run_ref[...] += jnp.sum(one_hot, axis=0, keepdims=True)
    count_ref[...] = run_ref[...]

    meta = jnp.zeros(logits.shape, F32)
    for at, val in ((META_E, i1 - N_GROUPS), (META_E + 1, i2 - N_GROUPS), (META_GATE, gate1),
                    (META_GATE + 1, gate2), (META_RANK, rank1), (META_RANK + 1, rank2)):
        meta = jnp.where(lane == at, val, meta)
    meta_ref[...] = meta
    meta_t_ref[...] = jnp.transpose(meta)[:META_ROWS]


def _out_proj_call(h, ya, yd, yc, wo_bf, gain, r_hi, r_lo, r_bias):
    n_tok, d_model = h.shape
    tm = ROW_TILE
    row = lambda w: pl.BlockSpec((tm, w), lambda i: (i, 0))
    full = lambda a: pl.BlockSpec(a.shape, lambda i: (0, 0))
    gain = gain.reshape(1, d_model)
    ltri = (jnp.arange(tm)[:, None] > jnp.arange(tm)[None, :]).astype(BF16)
    return pl.pallas_call(
        _out_proj_kernel,
        grid=(n_tok // tm,),
        in_specs=[row(d_model), row(ya.shape[1]), row(yd.shape[1]), row(yc.shape[1]),
                  full(wo_bf), full(gain), full(r_hi), full(r_lo), full(r_bias), full(ltri)],
        out_specs=[row(d_model), row(d_model), row(LANES),
                   pl.BlockSpec((META_ROWS, tm), lambda i: (0, i)),
                   pl.BlockSpec((1, LANES), lambda i: (0, 0))],
        out_shape=[jax.ShapeDtypeStruct((n_tok, d_model), F32),
                   jax.ShapeDtypeStruct((n_tok, d_model), F32),
                   jax.ShapeDtypeStruct((n_tok, LANES), F32),
                   jax.ShapeDtypeStruct((META_ROWS, n_tok), F32),
                   jax.ShapeDtypeStruct((1, LANES), F32)],
        scratch_shapes=[pltpu.VMEM((1, LANES), F32)],
        compiler_params=_cparams(1), name="out_proj_router")(
            h, ya, yd, yc, wo_bf, gain, r_hi, r_lo, r_bias, ltri)


def _dispatch_plan(meta_t, count_row, tb, tm):
    n_tok = meta_t.shape[1]
    counts = count_row[0, N_GROUPS:N_GROUPS + N_EXPERTS].astype(jnp.int32)
    padded = (counts + tb - 1) // tb * tb
    e = jnp.arange(N_EXPERTS)
    pend = jnp.sum(jnp.where(e[:, None] >= e[None, :], padded[None, :], 0), axis=1)
    pstart = pend - padded
    n_rows = n_tok * TOP_K_INNER + N_EXPERTS * tb
    block_start = jnp.arange(n_rows // tb, dtype=jnp.int32) * tb
    blk_expert = jnp.minimum(jnp.sum((pend[None, :] <= block_start[:, None]).astype(jnp.int32), axis=1),
                             N_EXPERTS - 1)
    n_used = pend[-1:] // tb
    expert = meta_t[META_E:META_E + TOP_K_INNER].astype(jnp.int32)
    rank = meta_t[META_RANK:META_RANK + TOP_K_INNER].astype(jnp.int32)
    seg_start = jnp.sum(jnp.where(expert[..., None] == e, pstart, 0), axis=-1)
    idx = jnp.transpose(seg_start + rank).reshape(n_tok // tm, 1, tm * TOP_K_INNER)
    return pend.astype(jnp.int32), blk_expert, n_used, idx, n_rows


def _for_each_assignment(tm, fn):
    def body(r, c):
        for k in range(TOP_K_INNER):
            fn(r, k)
        return c

    lax.fori_loop(0, tm, body, 0, unroll=8)


def _dispatch_kernel(pend_ref, idx_ref, f_ref, xs_ref, zero_ref, row_sem, zero_sem, *, tb):
    tm = f_ref.shape[0]

    @pl.when(pl.program_id(0) == 0)
    def _():
        zero_ref[...] = jnp.zeros(zero_ref.shape, F32)

        def block_copy(start):
            return pltpu.make_async_copy(zero_ref, xs_ref.at[pl.ds(pl.multiple_of(start, tb), tb)],
                                         zero_sem)

        for e in range(N_EXPERTS):
            block_copy(jnp.maximum(pend_ref[e] - tb, 0)).start()
        for e in range(N_EXPERTS):
            block_copy(jnp.maximum(pend_ref[e] - tb, 0)).wait()

        def fill(b, c):
            block_copy(b * tb).start()
            block_copy(b * tb).wait()
            return c

        first_unused = lax.shift_right_logical(pend_ref[N_EXPERTS - 1], tb.bit_length() - 1)
        lax.fori_loop(first_unused, xs_ref.shape[0] // tb, fill, 0)

    def row_copy(r, k):
        return pltpu.make_async_copy(f_ref.at[pl.ds(r, 1)], xs_ref.at[pl.ds(idx_ref[0, 0, r * TOP_K_INNER + k], 1)],
                                     row_sem)

    _for_each_assignment(tm, lambda r, k: row_copy(r, k).start())
    _for_each_assignment(tm, lambda r, k: row_copy(r, k).wait())


def _dispatch_call(f, pend, idx, n_rows):
    n_tok, d_model = f.shape
    tm = idx.shape[2] // TOP_K_INNER
    tb = MOE_TILE
    assert tb & (tb - 1) == 0
    grid_spec = pltpu.PrefetchScalarGridSpec(
        num_scalar_prefetch=1,
        grid=(n_tok // tm,),
        in_specs=[pl.BlockSpec((1,) + idx.shape[1:], lambda i, pe: (i, 0, 0),
                               memory_space=pltpu.SMEM),
                  pl.BlockSpec((tm, d_model), lambda i, pe: (i, 0))],
        out_specs=pl.BlockSpec(memory_space=pl.ANY),
        scratch_shapes=[pltpu.VMEM((tb, d_model), F32), pltpu.SemaphoreType.DMA(()),
                        pltpu.SemaphoreType.DMA(())])
    return pl.pallas_call(
        functools.partial(_dispatch_kernel, tb=tb), grid_spec=grid_spec,
        out_shape=jax.ShapeDtypeStruct((n_rows, d_model), F32),
        compiler_params=_cparams(1), name="moe_dispatch")(pend, idx, f)


def _expert_kernel(blk_expert_ref, n_used_ref, x_ref, wg_ref, wu_ref, wd_ref, y_ref,
                   wg_bf, wu_bf, wd_bf):
    i = pl.program_id(0)
    used = i < n_used_ref[0]
    e = blk_expert_ref[i]
    e_prev = blk_expert_ref[jnp.maximum(i - 1, 0)]

    @pl.when(used & ((i == 0) | (e != e_prev)))
    def _():
        wg_bf[...] = wg_ref[0, 0].astype(BF16)
        wu_bf[...] = wu_ref[0, 0].astype(BF16)
        wd_bf[...] = wd_ref[0, 0].astype(BF16)

    @pl.when(used)
    def _():
        x = x_ref[...].astype(BF16)
        g = jnp.dot(x, wg_bf[...], preferred_element_type=F32)
        u = jnp.dot(x, wu_bf[...], preferred_element_type=F32)
        hid = (g / (1.0 + jnp.exp(-g)) * u).astype(BF16)
        y_ref[...] = jnp.dot(hid, wd_bf[...], preferred_element_type=F32)

    @pl.when(jnp.logical_not(used))
    def _():
        y_ref[...] = jnp.zeros(y_ref.shape, F32)


def _expert_call(xs, blk_expert, n_used, w_gate, w_up, w_down, layer):
    n_rows, d_model = xs.shape
    d_exp = w_gate.shape[3]
    tb = MOE_TILE
    blk = lambda i, be, nu: jnp.minimum(i, nu[0] - 1)
    weight = lambda shape: pl.BlockSpec(
        (1, 1) + shape, lambda i, be, nu: (layer, be[blk(i, be, nu)], 0, 0))
    grid_spec = pltpu.PrefetchScalarGridSpec(
        num_scalar_prefetch=2,
        grid=(n_rows // tb,),
        in_specs=[pl.BlockSpec((tb, d_model), lambda i, be, nu: (blk(i, be, nu), 0)),
                  weight((d_model, d_exp)), weight((d_model, d_exp)), weight((d_exp, d_model))],
        out_specs=pl.BlockSpec((tb, d_model), lambda i, be, nu: (i, 0)),
        scratch_shapes=[pltpu.VMEM((d_model, d_exp), BF16), pltpu.VMEM((d_model, d_exp), BF16),
                        pltpu.VMEM((d_exp, d_model), BF16)])
    return pl.pallas_call(
        _expert_kernel, grid_spec=grid_spec,
        out_shape=jax.ShapeDtypeStruct(xs.shape, F32),
        compiler_params=_cparams(1), name="expert_mlp")(
            blk_expert, n_used, xs, w_gate, w_up, w_down)


def _combine_kernel(idx_ref, h_ref, meta_ref, y_ref, o_ref, ybuf_ref, sem):
    tm = h_ref.shape[0]

    def row_copy(r, k):
        return pltpu.make_async_copy(y_ref.at[pl.ds(idx_ref[0, 0, r * TOP_K_INNER + k], 1)],
                                     ybuf_ref.at[k, pl.ds(r, 1)], sem)

    _for_each_assignment(tm, lambda r, k: row_copy(r, k).start())
    _for_each_assignment(tm, lambda r, k: row_copy(r, k).wait())
    meta = meta_ref[...]
    out = h_ref[...]
    for k in range(TOP_K_INNER):
        out = out + meta[:, META_GATE + k:META_GATE + k + 1] * ybuf_ref[k]
    o_ref[...] = out


def _combine_call(h2, meta, y_rows, idx):
    n_tok, d_model = h2.shape
    tm = idx.shape[2] // TOP_K_INNER
    return pl.pallas_call(
        _combine_kernel,
        grid=(n_tok // tm,),
        in_specs=[pl.BlockSpec((1,) + idx.shape[1:], lambda i: (i, 0, 0), memory_space=pltpu.SMEM),
                  pl.BlockSpec((tm, d_model), lambda i: (i, 0)),
                  pl.BlockSpec((tm, LANES), lambda i: (i, 0)),
                  pl.BlockSpec(memory_space=pl.ANY)],
        out_specs=pl.BlockSpec((tm, d_model), lambda i: (i, 0)),
        out_shape=jax.ShapeDtypeStruct((n_tok, d_model), F32),
        scratch_shapes=[pltpu.VMEM((TOP_K_INNER, tm, d_model), F32), pltpu.SemaphoreType.DMA(())],
        compiler_params=_cparams(1), name="moe_combine")(idx, h2, meta, y_rows)


def _moe(h2, f, meta, meta_t, count_row, w_gate, w_up, w_down, layer):
    pend, blk_expert, n_used, idx, n_rows = _dispatch_plan(meta_t, count_row, MOE_TILE, ROW_TILE)
    xs = _dispatch_call(f, pend, idx, n_rows)
    y_rows = _expert_call(xs, blk_expert, n_used, w_gate, w_up, w_down, layer)
    return _combine_call(h2, meta, y_rows, idx)


def _proj_col_gain(diff_qk_norm, ca_qk_norm, d_model):
    sb_w, df_w = d_model // 4, d_model // 2
    ones = lambda n: jnp.ones((n,), F32)
    rep = lambda g, n: jnp.tile(g.astype(F32), n // HEAD_DIM)
    return jnp.concatenate([
        ones(3 * sb_w),
        rep(diff_qk_norm[0], df_w) * Q_FOLD, rep(diff_qk_norm[1], df_w), ones(df_w),
        rep(ca_qk_norm[0], sb_w) * Q_FOLD, rep(ca_qk_norm[1], sb_w), ones(sb_w)]).reshape(1, -1)


def kernel(x, t5_bias, attn_norm, w_in, diff_qk_norm, diff_lambda, diff_subln, ca_qk_norm,
           ca_rel_bias, w_out, ffn_norm, router_group_w, router_group_b, router_expert_w,
           router_expert_b, expert_w_gate, expert_w_up, expert_w_down):
    batch, seq, d_model = x.shape
    depth = w_in.shape[0]
    n_tok = batch * seq
    lane_group = jnp.arange(LANES) // HEAD_DIM
    group_mean = ((lane_group[:, None] == lane_group[None, :]).astype(F32) / HEAD_DIM).astype(BF16)

    h = x.reshape(n_tok, d_model)
    for l in range(depth):
        lam_init = 0.8 - 0.6 * math.exp(-0.3 * l)
        col_gain = _proj_col_gain(diff_qk_norm[l], ca_qk_norm[l], d_model)
        proj = _proj_call(h, attn_norm[l], w_in[l].astype(BF16), col_gain, group_mean)
        ya, yd, yc = _attention_calls(proj, batch, t5_bias, diff_qk_norm[l], diff_lambda[l],
                                      diff_subln[l], ca_qk_norm[l], ca_rel_bias[l], lam_init)
        r_w = jnp.concatenate([router_group_w[l], router_expert_w[l]], axis=1).astype(F32)
        r_w = jnp.pad(r_w, ((0, 0), (0, LANES - r_w.shape[1])))
        r_hi, r_lo = _split_bf16(r_w)
        r_b = jnp.concatenate([router_group_b[l], router_expert_b[l]]).astype(F32)
        r_b = jnp.pad(r_b, (0, LANES - r_b.shape[0])).reshape(1, LANES)
        h2, f, meta, meta_t, count_row = _out_proj_call(h, ya, yd, yc, w_out[l].astype(BF16),
                                                        ffn_norm[l], r_hi, r_lo, r_b)
        h = _moe(h2, f, meta, meta_t, count_row, expert_w_gate, expert_w_up, expert_w_down, l)
    return h.reshape(batch, seq, d_model)
```

```python
import functools
import math

import jax
import jax.numpy as jnp
from jax import lax
from jax.experimental import pallas as pl
from jax.experimental.pallas import tpu as pltpu

F32 = jnp.float32
BF16 = jnp.bfloat16

LANES = 128
MXU_WIDTH = 256
HEAD_DIM = 64
CHUNK = 64
CA_LEFT_CHUNKS = 8
CA_REL_CLIP = 128
T5_BUCKETS = 32
T5_MAX_DIST = 256
N_GROUPS = 4
EXPERTS_PER_GROUP = 8
N_EXPERTS = N_GROUPS * EXPERTS_PER_GROUP
TOP_K_INNER = 2
EPS = 1e-6
NEG = -1e30
LOG2E = math.log2(math.e)
Q_FOLD = HEAD_DIM ** -0.5 * LOG2E

ROW_TILE = 512
ATT_TILE = 256
MOE_TILE = 512
VMEM_LIMIT = 48 * 1024 * 1024


def _cparams(n_axes):
    return pltpu.CompilerParams(dimension_semantics=("arbitrary",) * n_axes,
                                vmem_limit_bytes=VMEM_LIMIT)


def _split_bf16(x):
    hi = x.astype(BF16)
    lo = (x - hi.astype(F32)).astype(BF16)
    return hi, lo


def _col_kind(col, d_model):
    sb_w, df_w = d_model // 4, d_model // 2
    sb_end = 3 * sb_w
    df_end = sb_end + 3 * df_w
    if col < sb_w:
        return "scale"
    if col < sb_end:
        return "plain"
    if col < sb_end + 2 * df_w:
        return "norm"
    if col < df_end:
        return "plain"
    if col < df_end + 2 * sb_w:
        return "norm"
    return "plain"


def _proj_kernel(h_ref, g_ref, w_ref, cg_ref, gm_ref, o_ref, *, d_model, col_chunk):
    x = h_ref[...]
    ms = jnp.mean(x * x, axis=-1, keepdims=True)
    xn = (x * lax.rsqrt(ms + EPS) * g_ref[...]).astype(BF16)
    n_out = w_ref.shape[1]
    gm = gm_ref[...]
    width = gm.shape[0]
    for c0 in range(0, n_out, col_chunk):
        acc = jnp.dot(xn, w_ref[:, c0:c0 + col_chunk], preferred_element_type=F32)
        for s0 in range(0, col_chunk, width):
            col = c0 + s0
            blk = acc[:, s0:s0 + width]
            kind = _col_kind(col, d_model)
            assert all(_col_kind(c, d_model) == kind for c in range(col, col + width, LANES))
            if kind == "norm":
                hi, lo = _split_bf16(blk * blk)
                msq = (jnp.dot(hi, gm, preferred_element_type=F32)
                       + jnp.dot(lo, gm, preferred_element_type=F32))
                blk = blk * lax.rsqrt(msq + EPS) * cg_ref[:, col:col + width]
            elif kind == "scale":
                blk = blk * Q_FOLD
            o_ref[:, col:col + width] = blk.astype(BF16)


def _proj_call(h, gain, w_bf, col_gain, group_mean):
    n_tok, d_model = h.shape
    n_out = w_bf.shape[1]
    tm = ROW_TILE
    return pl.pallas_call(
        functools.partial(_proj_kernel, d_model=d_model, col_chunk=512),
        grid=(n_tok // tm,),
        in_specs=[pl.BlockSpec((tm, d_model), lambda i: (i, 0)),
                  pl.BlockSpec((1, d_model), lambda i: (0, 0)),
                  pl.BlockSpec((d_model, n_out), lambda i: (0, 0)),
                  pl.BlockSpec((1, n_out), lambda i: (0, 0)),
                  pl.BlockSpec(group_mean.shape, lambda i: (0, 0))],
        out_specs=pl.BlockSpec((tm, n_out), lambda i: (i, 0)),
        out_shape=jax.ShapeDtypeStruct((n_tok, n_out), BF16),
        compiler_params=_cparams(1), name="norm_in_proj")(
            h, gain.reshape(1, d_model), w_bf, col_gain, group_mean)


def _stack_halves(q):
    lane = lax.broadcasted_iota(jnp.int32, q.shape, 1)
    zero = jnp.zeros_like(q)
    return jnp.concatenate([jnp.where(lane < HEAD_DIM, q, zero),
                            jnp.where(lane >= HEAD_DIM, q, zero)], axis=0)


def _scores(q2, k):
    return lax.dot_general(q2, k, (((1,), (1,)), ((), ())), preferred_element_type=F32)


def _lane_tile(x, width):
    reps = width // LANES
    return x if reps == 1 else jnp.concatenate([x] * reps, axis=1)


def _lane_fold(x, op):
    out = x[:, :LANES]
    for c in range(LANES, x.shape[1], LANES):
        out = op(out, x[:, c:c + LANES])
    return out


def _rows_tile(ref, j, tk):
    return ref[0, pl.ds(pl.multiple_of(j * tk, tk), tk), :]


def _for_each(n, fn, ways):
    def group(g, c):
        for w in range(ways):
            fn(g * ways + w)
        return c

    lax.fori_loop(0, n // ways, group, 0)
    part = ways // 2
    while part >= 1:
        start = n // (2 * part) * (2 * part)

        def tail(start=start, part=part):
            for w in range(part):
                fn(start + w)

        pl.when(n % (2 * part) >= part)(tail)
        part //= 2


def _merge_halves(y, tq):
    lane = lax.broadcasted_iota(jnp.int32, (tq, LANES), 1)
    return jnp.where(lane < HEAD_DIM, y[:tq], y[tq:])


SOFTPLUS_CLAMP = 126.0
F32_EXP2_UNDERFLOW = 152.0


def _sb_attend(i, q_ref, k_ref, v_ref, u_ref, o_ref, acc_ref, skipped_ref, *, tq):
    q2 = _stack_halves(q_ref[0])
    rows = 2 * tq
    acc_ref[...] = jnp.zeros(acc_ref.shape, F32)
    skipped_ref[...] = jnp.zeros(skipped_ref.shape, F32)

    def tile(j, diagonal):
        z = _scores(q2, _rows_tile(k_ref, j, tq))
        sp = jnp.maximum(jnp.log(1.0 + jnp.exp2(jnp.minimum(z, SOFTPLUS_CLAMP))) * LOG2E, z)
        if diagonal:
            row = lax.broadcasted_iota(jnp.int32, (rows, tq), 0)
            col = lax.broadcasted_iota(jnp.int32, (rows, tq), 1)
            earlier = col < jnp.where(row >= tq, row - tq, row)
            sp = jnp.where(earlier, sp, 0.0)
        hi, lo = _split_bf16(sp)
        suffix = jnp.dot(jnp.concatenate([hi, lo], axis=1), u_ref[...],
                         preferred_element_type=F32)
        skipped = skipped_ref[...]
        w = jnp.exp2(z - suffix - _lane_tile(skipped, tq))
        if diagonal:
            w = jnp.where(earlier, w, 0.0)
        acc_ref[...] += jnp.dot(w.astype(BF16), _rows_tile(v_ref, j, tq),
                                preferred_element_type=F32)
        skipped = skipped + jnp.broadcast_to(suffix[:, :1], skipped.shape)
        skipped_ref[...] = skipped
        return jnp.min(skipped)

    def more(state):
        t, least_skipped = state
        return (t <= i) & (least_skipped < F32_EXP2_UNDERFLOW)

    def step(state):
        t, _ = state
        return t + 1, tile(i - t, False)

    lax.while_loop(more, step, (jnp.int32(1), tile(i, True)))
    o_ref[0] = _merge_halves(acc_ref[...], tq).astype(o_ref.dtype)


BIAS_FAR, LOGIT_MAX, LOGIT_SPREAD = 0, 1, 2
SAFE_EXP2_RANGE = 80.0


def _qk_norm_reach(qk_norm):
    g = jnp.abs(qk_norm.astype(F32))
    return 1.01 * HEAD_DIM * Q_FOLD * jnp.max(g[0]) * jnp.max(g[1])


def _logit_stats(far, bias_max, bias_min, reach):
    return jnp.stack([far, bias_max + reach, bias_max - bias_min + 2.0 * reach])


def _diff_attend(h, i, stats_ref, q_ref, k_ref, v_ref, near_ref, lam_ref, subln_ref, o_ref,
                 s_ref, p_ref, m_ref, l_ref, acc_ref, *, tq, lam_init):
    q2 = _stack_halves(q_ref[0])
    rows = 2 * tq
    far = stats_ref[BIAS_FAR, h]
    bounded = stats_ref[LOGIT_SPREAD, h] <= SAFE_EXP2_RANGE
    lmb = lam_ref[...]
    lam = (jnp.exp(jnp.sum(lmb[0:1] * lmb[1:2], axis=-1, keepdims=True))
           - jnp.exp(jnp.sum(lmb[2:3] * lmb[3:4], axis=-1, keepdims=True)) + lam_init)
    l_ref[...] = jnp.zeros(l_ref.shape, F32)
    acc_ref[...] = jnp.zeros(acc_ref.shape, F32)

    def biased_scores(j, bias):
        s = _scores(q2, _rows_tile(k_ref, j, tq))
        if bias is None:
            return s
        return (s.reshape(2, tq, tq) + bias[None]).reshape(rows, tq)

    @pl.when(bounded)
    def _():
        shift = stats_ref[LOGIT_MAX, h]

        def probs(j, bias, shift):
            p = jnp.exp2(biased_scores(j, bias) - shift)
            l_ref[...] += _lane_fold(p, jnp.add)
            p_ref[j] = p.astype(BF16)

        probs(i, near_ref[0, 0], shift)
        pl.when(i >= 1)(lambda: probs(i - 1, near_ref[0, 1], shift))
        _for_each(jnp.maximum(i - 1, 0), lambda j: probs(j, None, shift - far), 4)

        inv_l = 1.0 / jnp.sum(l_ref[...], axis=1, keepdims=True)
        m_ref[0:tq] = jnp.broadcast_to(inv_l[:tq], (tq, LANES))
        m_ref[tq:rows] = jnp.broadcast_to(-lam * inv_l[tq:], (tq, LANES))

        def weigh(j):
            p = p_ref[j]
            w = (p[:tq].astype(F32) * _lane_tile(m_ref[0:tq], tq)
                 + p[tq:].astype(F32) * _lane_tile(m_ref[tq:rows], tq))
            acc_ref[0:tq] += jnp.dot(w.astype(BF16), _rows_tile(v_ref, j, tq),
                                     preferred_element_type=F32)

        _for_each(i + 1, weigh, 4)

    @pl.when(jnp.logical_not(bounded))
    def _():
        m_ref[...] = jnp.full(m_ref.shape, NEG, F32)

        def score(j, bias):
            s = biased_scores(j, bias)
            if bias is None:
                s = s + far
            s_ref[j] = s
            m_ref[...] = jnp.maximum(m_ref[...], _lane_fold(s, jnp.maximum))

        score(i, near_ref[0, 0])
        pl.when(i >= 1)(lambda: score(i - 1, near_ref[0, 1]))
        _for_each(jnp.maximum(i - 1, 0), lambda j: score(j, None), 4)
        m_ref[...] = jnp.broadcast_to(jnp.max(m_ref[...], axis=1, keepdims=True), m_ref.shape)

        def weigh(j):
            p = jnp.exp2(s_ref[j] - _lane_tile(m_ref[...], tq))
            l_ref[...] += _lane_fold(p, jnp.add)
            acc_ref[...] += jnp.dot(p.astype(BF16), _rows_tile(v_ref, j, tq),
                                    preferred_element_type=F32)

        _for_each(i + 1, weigh, 4)
        y = acc_ref[...] / jnp.sum(l_ref[...], axis=1, keepdims=True)
        acc_ref[0:tq] = y[:tq] - lam * y[tq:]

    y = acc_ref[0:tq]
    ms = jnp.mean(y * y, axis=-1, keepdims=True)
    y = y * lax.rsqrt(ms + EPS) * subln_ref[...] * (1.0 - lam_init)
    o_ref[0] = y.astype(o_ref.dtype)


def _ca_attend(h, i, stats_ref, q_ref, k_ref, v_ref, bias_ref, o_ref, *, tq, n_near):
    q2 = _stack_halves(q_ref[0])
    bounded = stats_ref[LOGIT_SPREAD, h] <= SAFE_EXP2_RANGE

    def scores(d):
        j = jnp.maximum(i - d, 0)
        s = _scores(q2, _rows_tile(k_ref, j, tq)) + bias_ref[0, d]
        if d > 0:
            s = s + jnp.where(i >= d, 0.0, NEG)
        return j, s

    def attend(tiles, shift):
        l = None
        acc = None
        for j, s in tiles:
            p = jnp.exp2(s - shift)
            pv = jnp.dot(p.astype(BF16), _rows_tile(v_ref, j, tq), preferred_element_type=F32)
            psum = _lane_fold(p, jnp.add)
            l = psum if l is None else l + psum
            acc = pv if acc is None else acc + pv
        y = acc / jnp.sum(l, axis=1, keepdims=True)
        o_ref[0] = _merge_halves(y, tq).astype(o_ref.dtype)

    @pl.when(bounded)
    def _():
        attend((scores(d) for d in range(n_near)), stats_ref[LOGIT_MAX, h])

    @pl.when(jnp.logical_not(bounded))
    def _():
        tiles = [scores(d) for d in range(n_near)]
        m = _lane_fold(tiles[0][1], jnp.maximum)
        for _, s in tiles[1:]:
            m = jnp.maximum(m, _lane_fold(s, jnp.maximum))
        m = jnp.broadcast_to(jnp.max(m, axis=1, keepdims=True), m.shape)
        attend(tiles, _lane_tile(m, tq))


def _mixers_kernel(*refs, tq, lam_init, n_near, df_per_step):
    refs = list(refs)
    take = lambda n: [refs.pop(0) for _ in range(n)]
    diff_stats_ref, ca_stats_ref = take(2)
    sb_in = take(4)
    diff_in = [take(4) for _ in range(df_per_step)]
    lam_ref, subln_ref = take(2)
    ca_in = take(4)
    ya_ref, yd_ref, yc_ref = take(3)
    s_ref, p_ref, m_ref, l_ref, acc_ref, skipped_ref = refs
    g = pl.program_id(1)
    i = pl.program_id(2)
    _sb_attend(i, *sb_in, ya_ref, acc_ref, skipped_ref, tq=tq)
    for e, (q_ref, k_ref, v_ref, near_ref) in enumerate(diff_in):
        _diff_attend(df_per_step * g + e, i, diff_stats_ref, q_ref, k_ref, v_ref, near_ref,
                     lam_ref, subln_ref, yd_ref.at[:, :, pl.ds(e * LANES, LANES)],
                     s_ref, p_ref, m_ref, l_ref, acc_ref, tq=tq, lam_init=lam_init)
    _ca_attend(g, i, ca_stats_ref, *ca_in, yc_ref, tq=tq, n_near=n_near)


def _t5_bucket(rel):
    nb = T5_BUCKETS // 2
    max_exact = nb // 2
    ret = jnp.where(rel > 0, nb, 0)
    n = jnp.abs(rel)
    large = max_exact + (jnp.log(jnp.maximum(n, 1).astype(F32) / max_exact)
                         / math.log(T5_MAX_DIST / max_exact) * (nb - max_exact)).astype(jnp.int32)
    large = jnp.minimum(large, nb - 1)
    return ret + jnp.where(n < max_exact, n, large)


def _toeplitz_tile(bias_of_rel, tq, d):
    span = 2 * tq
    x = jnp.arange(span, dtype=jnp.int32)
    x = jnp.where(x < tq, x, x - span)
    vec = jnp.transpose(bias_of_rel(x - d * tq)).astype(F32) * LOG2E
    flat = jnp.tile(vec, (1, tq))[:, :tq * (span - 1)]
    return flat.reshape(vec.shape[0], tq, span - 1)[:, :, :tq]


def _diff_bias_tables(t5_bias, qk_norm, tq):
    assert tq >= T5_MAX_DIST and tq % CHUNK == 0
    r = jnp.arange(tq)[:, None]
    c = jnp.arange(tq)[None, :]
    bias_of_rel = lambda rel: t5_bias[_t5_bucket(rel)]
    diag = jnp.where((c // CHUNK) <= (r // CHUNK), _toeplitz_tile(bias_of_rel, tq, 0), NEG)
    scaled = t5_bias.astype(F32) * LOG2E
    far = scaled[_t5_bucket(jnp.int32(-2 * tq))]
    stats = _logit_stats(far, jnp.max(scaled, axis=0), jnp.min(scaled, axis=0),
                         _qk_norm_reach(qk_norm))
    return jnp.stack([diag, _toeplitz_tile(bias_of_rel, tq, 1)], axis=1), stats


def _ca_bias_tables(rel_bias, qk_norm, tq):
    assert (CA_LEFT_CHUNKS * CHUNK) % tq == 0 and tq % CHUNK == 0
    n_near = CA_LEFT_CHUNKS * CHUNK // tq + 1
    r = jnp.arange(tq)[:, None]
    c = jnp.arange(tq)[None, :]
    bias_of_rel = lambda rel: rel_bias[jnp.clip(rel, -CA_REL_CLIP, CA_REL_CLIP) + CA_REL_CLIP]
    tiles = []
    for d in range(n_near):
        gap = d * (tq // CHUNK) + r // CHUNK - c // CHUNK
        tiles.append(jnp.where((gap >= 0) & (gap <= CA_LEFT_CHUNKS),
                               _toeplitz_tile(bias_of_rel, tq, d), NEG))
    t = jnp.stack(tiles, axis=1)
    n_heads = t.shape[0]
    t = t.reshape(n_heads // 2, 2, n_near, tq, tq).transpose(0, 2, 1, 3, 4)
    scaled = (rel_bias.astype(F32) * LOG2E).reshape(rel_bias.shape[0], n_heads // 2, 2)
    top = jnp.max(scaled, axis=(0, 2))
    stats = _logit_stats(jnp.zeros_like(top), top, jnp.min(scaled, axis=(0, 2)),
                         _qk_norm_reach(qk_norm))
    return t.reshape(n_heads // 2, n_near, 2 * tq, tq), stats, n_near


def _attention_calls(proj, batch, t5_bias, diff_qk_norm, diff_lambda, diff_subln, ca_qk_norm,
                     ca_rel_bias, lam_init):
    n_tok, n_proj = proj.shape
    seq = n_tok // batch
    d_model = n_proj // 3
    sb_w = d_model // 4
    proj3 = proj.reshape(batch, seq, n_proj)
    tq = ATT_TILE
    nq = seq // tq
    sb_blocks = sb_w // LANES
    df_heads = (d_model // 2) // LANES
    sb_q0 = 0
    df_q0 = 3 * sb_blocks
    ca_q0 = df_q0 + 3 * df_heads

    assert df_heads % sb_blocks == 0
    df_per_step = df_heads // sb_blocks

    def qkv_specs(q0, stride, per_step=1, e=0):
        col = lambda g: q0 + per_step * g + e
        return [pl.BlockSpec((1, tq, LANES), lambda b, g, i: (b, i, col(g))),
                pl.BlockSpec((1, seq, LANES), lambda b, g, i: (b, 0, stride + col(g))),
                pl.BlockSpec((1, seq, LANES), lambda b, g, i: (b, 0, 2 * stride + col(g)))]

    smem = pl.BlockSpec(memory_space=pltpu.SMEM)
    out_spec = lambda w: pl.BlockSpec((1, tq, w), lambda b, g, i: (b, i, g))
    stacked = pltpu.VMEM((2 * tq, LANES), F32)

    lower = (jnp.arange(tq)[:, None] >= jnp.arange(tq)[None, :]).astype(BF16)
    u = jnp.concatenate([lower, lower], axis=0)
    near, bias_stats = _diff_bias_tables(t5_bias, diff_qk_norm, tq)
    ca_bias, ca_stats, n_near = _ca_bias_tables(ca_rel_bias, ca_qk_norm, tq)

    in_specs = [smem, smem] + qkv_specs(sb_q0, sb_blocks) + [
        pl.BlockSpec(u.shape, lambda b, g, i: (0, 0))]
    args = [bias_stats, ca_stats, proj3, proj3, proj3, u]
    for e in range(df_per_step):
        in_specs += qkv_specs(df_q0, df_heads, df_per_step, e) + [
            pl.BlockSpec((1, 2, tq, tq), lambda b, g, i, e=e: (df_per_step * g + e, 0, 0, 0))]
        args += [proj3, proj3, proj3, near]
    in_specs += [pl.BlockSpec((4, HEAD_DIM), lambda b, g, i: (0, 0)),
                 pl.BlockSpec((1, LANES), lambda b, g, i: (0, 0))]
    args += [diff_lambda.astype(F32), diff_subln.astype(F32).reshape(1, LANES)]
    in_specs += qkv_specs(ca_q0, sb_blocks) + [
        pl.BlockSpec((1, n_near, 2 * tq, tq), lambda b, g, i: (g, 0, 0, 0))]
    args += [proj3, proj3, proj3, ca_bias]

    ya, yd, yc = pl.pallas_call(
        functools.partial(_mixers_kernel, tq=tq, lam_init=lam_init, n_near=n_near,
                          df_per_step=df_per_step),
        grid=(batch, sb_blocks, nq),
        in_specs=in_specs,
        out_specs=[out_spec(LANES), out_spec(df_per_step * LANES), out_spec(LANES)],
        out_shape=[jax.ShapeDtypeStruct((batch, seq, sb_w), BF16),
                   jax.ShapeDtypeStruct((batch, seq, d_model // 2), BF16),
                   jax.ShapeDtypeStruct((batch, seq, sb_w), BF16)],
        scratch_shapes=[pltpu.VMEM((nq, 2 * tq, tq), F32), pltpu.VMEM((nq, 2 * tq, tq), BF16),
                        stacked, stacked, stacked, stacked],
        compiler_params=_cparams(3), name="attention_mixers")(*args)
    return (ya.reshape(n_tok, sb_w), yd.reshape(n_tok, d_model // 2), yc.reshape(n_tok, sb_w))


META_E, META_GATE, META_RANK = 0, 2, 4
META_ROWS = 8


def _row_min_lane(mask, lane_f):
    return jnp.min(jnp.where(mask, lane_f, float(LANES)), axis=1, keepdims=True)


def _out_proj_kernel(h_ref, ya_ref, yd_ref, yc_ref, wo_ref, g_ref, rhi_ref, rlo_ref, rb_ref,
                     ltri_ref, h2_ref, f_ref, meta_ref, meta_t_ref, count_ref, run_ref):
    wa = ya_ref.shape[1]
    wd = yd_ref.shape[1]
    h2 = (h_ref[...]
          + jnp.dot(ya_ref[...], wo_ref[0:wa, :], preferred_element_type=F32)
          + jnp.dot(yd_ref[...], wo_ref[wa:wa + wd, :], preferred_element_type=F32)
          + jnp.dot(yc_ref[...], wo_ref[wa + wd:, :], preferred_element_type=F32))
    h2_ref[...] = h2
    ms = jnp.mean(h2 * h2, axis=-1, keepdims=True)
    f = h2 * lax.rsqrt(ms + EPS) * g_ref[...]
    f_ref[...] = f
    f_hi, f_lo = _split_bf16(f)
    logits = (jnp.dot(f_hi, rhi_ref[...], preferred_element_type=F32)
              + jnp.dot(f_lo, rhi_ref[...], preferred_element_type=F32)
              + jnp.dot(f_hi, rlo_ref[...], preferred_element_type=F32)
              + rb_ref[...])

    lane = lax.broadcasted_iota(jnp.int32, logits.shape, 1)
    lane_f = lane.astype(F32)
    is_group = lane < N_GROUPS
    gl = jnp.where(is_group, logits, NEG)
    g_max = jnp.max(gl, axis=1, keepdims=True)
    g_sel = _row_min_lane(gl == g_max, lane_f)
    g_w = 1.0 / jnp.sum(jnp.where(is_group, jnp.exp(logits - g_max), 0.0), axis=1, keepdims=True)
    first_lane = float(N_GROUPS) + float(EXPERTS_PER_GROUP) * g_sel
    in_group = (lane_f >= first_lane) & (lane_f < first_lane + float(EXPERTS_PER_GROUP))
    el = jnp.where(in_group, logits, NEG)
    m1 = jnp.max(el, axis=1, keepdims=True)
    i1 = _row_min_lane(el == m1, lane_f)
    el = jnp.where(lane_f == i1, NEG, el)
    m2 = jnp.max(el, axis=1, keepdims=True)
    i2 = _row_min_lane(el == m2, lane_f)
    r = jnp.exp(m2 - m1)
    p1 = 1.0 / (1.0 + r)
    gate1 = g_w * p1
    gate2 = g_w * (r * p1)

    @pl.when(pl.program_id(0) == 0)
    def _():
        run_ref[...] = jnp.zeros(run_ref.shape, F32)

    hit1 = lane_f == i1
    hit2 = lane_f == i2
    one_hot = jnp.where(hit1 | hit2, 1.0, 0.0)
    pos = run_ref[...] + jnp.dot(ltri_ref[...], one_hot.astype(BF16), preferred_element_type=F32)
    rank1 = jnp.sum(jnp.where(hit1, pos, 0.0), axis=1, keepdims=True)
    rank2 = jnp.sum(jnp.where(hit2, pos, 0.0), axis=1, keepdims=True)
    run_ref[...] += jnp.sum(one_hot, axis=0, keepdims=True)
    count_ref[...] = run_ref[...]

    meta = jnp.zeros(logits.shape, F32)
    for at, val in ((META_E, i1 - N_GROUPS), (META_E + 1, i2 - N_GROUPS), (META_GATE, gate1),
                    (META_GATE + 1, gate2), (META_RANK, rank1), (META_RANK + 1, rank2)):
        meta = jnp.where(lane == at, val, meta)
    meta_ref[...] = meta
    meta_t_ref[...] = jnp.transpose(meta)[:META_ROWS]


def _out_proj_call(h, ya, yd, yc, wo_bf, gain, r_hi, r_lo, r_bias):
    n_tok, d_model = h.shape
    tm = ROW_TILE
    row = lambda w: pl.BlockSpec((tm, w), lambda i: (i, 0))
    full = lambda a: pl.BlockSpec(a.shape, lambda i: (0, 0))
    gain = gain.reshape(1, d_model)
    ltri = (jnp.arange(tm)[:, None] > jnp.arange(tm)[None, :]).astype(BF16)
    return pl.pallas_call(
        _out_proj_kernel,
        grid=(n_tok // tm,),
        in_specs=[row(d_model), row(ya.shape[1]), row(yd.shape[1]), row(yc.shape[1]),
                  full(wo_bf), full(gain), full(r_hi), full(r_lo), full(r_bias), full(ltri)],
        out_specs=[row(d_model), row(d_model), row(LANES),
                   pl.BlockSpec((META_ROWS, tm), lambda i: (0, i)),
                   pl.BlockSpec((1, LANES), lambda i: (0, 0))],
        out_shape=[jax.ShapeDtypeStruct((n_tok, d_model), F32),
                   jax.ShapeDtypeStruct((n_tok, d_model), F32),
                   jax.ShapeDtypeStruct((n_tok, LANES), F32),
                   jax.ShapeDtypeStruct((META_ROWS, n_tok), F32),
                   jax.ShapeDtypeStruct((1, LANES), F32)],
        scratch_shapes=[pltpu.VMEM((1, LANES), F32)],
        compiler_params=_cparams(1), name="out_proj_router")(
            h, ya, yd, yc, wo_bf, gain, r_hi, r_lo, r_bias, ltri)


def _dispatch_plan(meta_t, count_row, tb, tm):
    n_tok = meta_t.shape[1]
    counts = count_row[0, N_GROUPS:N_GROUPS + N_EXPERTS].astype(jnp.int32)
    padded = (counts + tb - 1) // tb * tb
    e = jnp.arange(N_EXPERTS)
    pend = jnp.sum(jnp.where(e[:, None] >= e[None, :], padded[None, :], 0), axis=1)
    pstart = pend - padded
    n_rows = n_tok * TOP_K_INNER + N_EXPERTS * tb
    block_start = jnp.arange(n_rows // tb, dtype=jnp.int32) * tb
    blk_expert = jnp.minimum(jnp.sum((pend[None, :] <= block_start[:, None]).astype(jnp.int32), axis=1),
                             N_EXPERTS - 1)
    n_used = pend[-1:] // tb
    expert = meta_t[META_E:META_E + TOP_K_INNER].astype(jnp.int32)
    rank = meta_t[META_RANK:META_RANK + TOP_K_INNER].astype(jnp.int32)
    seg_start = jnp.sum(jnp.where(expert[..., None] == e, pstart, 0), axis=-1)
    idx = jnp.transpose(seg_start + rank).reshape(n_tok // tm, 1, tm * TOP_K_INNER)
    return pend.astype(jnp.int32), blk_expert, n_used, idx, n_rows


def _for_each_assignment(tm, fn):
    def body(r, c):
        for k in range(TOP_K_INNER):
            fn(r, k)
        return c

    lax.fori_loop(0, tm, body, 0, unroll=8)


def _dispatch_kernel(pend_ref, idx_ref, f_ref, xs_ref, zero_ref, row_sem, zero_sem, *, tb):
    tm = f_ref.shape[0]

    @pl.when(pl.program_id(0) == 0)
    def _():
        zero_ref[...] = jnp.zeros(zero_ref.shape, F32)

        def block_copy(start):
            return pltpu.make_async_copy(zero_ref, xs_ref.at[pl.ds(pl.multiple_of(start, tb), tb)],
                                         zero_sem)

        for e in range(N_EXPERTS):
            block_copy(jnp.maximum(pend_ref[e] - tb, 0)).start()
        for e in range(N_EXPERTS):
            block_copy(jnp.maximum(pend_ref[e] - tb, 0)).wait()

        def fill(b, c):
            block_copy(b * tb).start()
            block_copy(b * tb).wait()
            return c

        first_unused = lax.shift_right_logical(pend_ref[N_EXPERTS - 1], tb.bit_length() - 1)
        lax.fori_loop(first_unused, xs_ref.shape[0] // tb, fill, 0)

    def row_copy(r, k):
        return pltpu.make_async_copy(f_ref.at[pl.ds(r, 1)], xs_ref.at[pl.ds(idx_ref[0, 0, r * TOP_K_INNER + k], 1)],
                                     row_sem)

    _for_each_assignment(tm, lambda r, k: row_copy(r, k).start())
    _for_each_assignment(tm, lambda r, k: row_copy(r, k).wait())


def _dispatch_call(f, pend, idx, n_rows):
    n_tok, d_model = f.shape
    tm = idx.shape[2] // TOP_K_INNER
    tb = MOE_TILE
    assert tb & (tb - 1) == 0
    grid_spec = pltpu.PrefetchScalarGridSpec(
        num_scalar_prefetch=1,
        grid=(n_tok // tm,),
        in_specs=[pl.BlockSpec((1,) + idx.shape[1:], lambda i, pe: (i, 0, 0),
                               memory_space=pltpu.SMEM),
                  pl.BlockSpec((tm, d_model), lambda i, pe: (i, 0))],
        out_specs=pl.BlockSpec(memory_space=pl.ANY),
        scratch_shapes=[pltpu.VMEM((tb, d_model), F32), pltpu.SemaphoreType.DMA(()),
                        pltpu.SemaphoreType.DMA(())])
    return pl.pallas_call(
        functools.partial(_dispatch_kernel, tb=tb), grid_spec=grid_spec,
        out_shape=jax.ShapeDtypeStruct((n_rows, d_model), F32),
        compiler_params=_cparams(1), name="moe_dispatch")(pend, idx, f)


def _expert_kernel(blk_expert_ref, n_used_ref, x_ref, wg_ref, wu_ref, wd_ref, y_ref,
                   wg_bf, wu_bf, wd_bf):
    i = pl.program_id(0)
    used = i < n_used_ref[0]
    e = blk_expert_ref[i]
    e_prev = blk_expert_ref[jnp.maximum(i - 1, 0)]

    @pl.when(used & ((i == 0) | (e != e_prev)))
    def _():
        wg_bf[...] = wg_ref[0, 0].astype(BF16)
        wu_bf[...] = wu_ref[0, 0].astype(BF16)
        wd_bf[...] = wd_ref[0, 0].astype(BF16)

    @pl.when(used)
    def _():
        x = x_ref[...].astype(BF16)
        g = jnp.dot(x, wg_bf[...], preferred_element_type=F32)
        u = jnp.dot(x, wu_bf[...], preferred_element_type=F32)
        hid = (g / (1.0 + jnp.exp(-g)) * u).astype(BF16)
        y_ref[...] = jnp.dot(hid, wd_bf[...], preferred_element_type=F32)

    @pl.when(jnp.logical_not(used))
    def _():
        y_ref[...] = jnp.zeros(y_ref.shape, F32)


def _expert_call(xs, blk_expert, n_used, w_gate, w_up, w_down, layer):
    n_rows, d_model = xs.shape
    d_exp = w_gate.shape[3]
    tb = MOE_TILE
    blk = lambda i, be, nu: jnp.minimum(i, nu[0] - 1)
    weight = lambda shape: pl.BlockSpec(
        (1, 1) + shape, lambda i, be, nu: (layer, be[blk(i, be, nu)], 0, 0))
    grid_spec = pltpu.PrefetchScalarGridSpec(
        num_scalar_prefetch=2,
        grid=(n_rows // tb,),
        in_specs=[pl.BlockSpec((tb, d_model), lambda i, be, nu: (blk(i, be, nu), 0)),
                  weight((d_model, d_exp)), weight((d_model, d_exp)), weight((d_exp, d_model))],
        out_specs=pl.BlockSpec((tb, d_model), lambda i, be, nu: (i, 0)),
        scratch_shapes=[pltpu.VMEM((d_model, d_exp), BF16), pltpu.VMEM((d_model, d_exp), BF16),
                        pltpu.VMEM((d_exp, d_model), BF16)])
    return pl.pallas_call(
        _expert_kernel, grid_spec=grid_spec,
        out_shape=jax.ShapeDtypeStruct(xs.shape, F32),
        compiler_params=_cparams(1), name="expert_mlp")(
            blk_expert, n_used, xs, w_gate, w_up, w_down)


def _combine_kernel(idx_ref, idx_next_ref, h_ref, meta_ref, y_ref, o_ref, ybuf_ref, sems):
    i = pl.program_id(0)
    tm = h_ref.shape[0]
    slot = i % 2

    def row_copy(index_ref, slot, r, k):
        src = index_ref[0, 0, r * TOP_K_INNER + k]
        return pltpu.make_async_copy(y_ref.at[pl.ds(src, 1)],
                                     ybuf_ref.at[slot, k, pl.ds(r, 1)], sems.at[slot])

    def request(index_ref, slot):
        _for_each_assignment(tm, lambda r, k: row_copy(index_ref, slot, r, k).start())

    pl.when(i == 0)(lambda: request(idx_ref, 0))
    pl.when(i + 1 < pl.num_programs(0))(lambda: request(idx_next_ref, 1 - slot))
    _for_each_assignment(tm, lambda r, k: row_copy(idx_ref, slot, r, k).wait())
    meta = meta_ref[...]
    out = h_ref[...]
    for k in range(TOP_K_INNER):
        out = out + meta[:, META_GATE + k:META_GATE + k + 1] * ybuf_ref[slot, k]
    o_ref[...] = out


def _combine_call(h2, meta, y_rows, idx):
    n_tok, d_model = h2.shape
    tm = idx.shape[2] // TOP_K_INNER
    n_tiles = n_tok // tm
    idx_block = (1,) + idx.shape[1:]
    return pl.pallas_call(
        _combine_kernel,
        grid=(n_tiles,),
        in_specs=[pl.BlockSpec(idx_block, lambda i: (i, 0, 0), memory_space=pltpu.SMEM),
                  pl.BlockSpec(idx_block, lambda i: (jnp.minimum(i + 1, n_tiles - 1), 0, 0),
                               memory_space=pltpu.SMEM),
                  pl.BlockSpec((tm, d_model), lambda i: (i, 0)),
                  pl.BlockSpec((tm, LANES), lambda i: (i, 0)),
                  pl.BlockSpec(memory_space=pl.ANY)],
        out_specs=pl.BlockSpec((tm, d_model), lambda i: (i, 0)),
        out_shape=jax.ShapeDtypeStruct((n_tok, d_model), F32),
        scratch_shapes=[pltpu.VMEM((2, TOP_K_INNER, tm, d_model), F32),
                        pltpu.SemaphoreType.DMA((2,))],
        compiler_params=_cparams(1), name="moe_combine")(idx, idx, h2, meta, y_rows)


def _moe(h2, f, meta, meta_t, count_row, w_gate, w_up, w_down, layer):
    pend, blk_expert, n_used, idx, n_rows = _dispatch_plan(meta_t, count_row, MOE_TILE, ROW_TILE)
    xs = _dispatch_call(f, pend, idx, n_rows)
    y_rows = _expert_call(xs, blk_expert, n_used, w_gate, w_up, w_down, layer)
    return _combine_call(h2, meta, y_rows, idx)


def _proj_col_gain(diff_qk_norm, ca_qk_norm, d_model):
    sb_w, df_w = d_model // 4, d_model // 2
    ones = lambda n: jnp.ones((n,), F32)
    rep = lambda g, n: jnp.tile(g.astype(F32), n // HEAD_DIM)
    return jnp.concatenate([
        ones(3 * sb_w),
        rep(diff_qk_norm[0], df_w) * Q_FOLD, rep(diff_qk_norm[1], df_w), ones(df_w),
        rep(ca_qk_norm[0], sb_w) * Q_FOLD, rep(ca_qk_norm[1], sb_w), ones(sb_w)]).reshape(1, -1)


def kernel(x, t5_bias, attn_norm, w_in, diff_qk_norm, diff_lambda, diff_subln, ca_qk_norm,
           ca_rel_bias, w_out, ffn_norm, router_group_w, router_group_b, router_expert_w,
           router_expert_b, expert_w_gate, expert_w_up, expert_w_down):
    batch, seq, d_model = x.shape
    depth = w_in.shape[0]
    n_tok = batch * seq
    lane_group = jnp.arange(MXU_WIDTH) // HEAD_DIM
    group_mean = ((lane_group[:, None] == lane_group[None, :]).astype(F32) / HEAD_DIM).astype(BF16)

    h = x.reshape(n_tok, d_model)
    for l in range(depth):
        lam_init = 0.8 - 0.6 * math.exp(-0.3 * l)
        col_gain = _proj_col_gain(diff_qk_norm[l], ca_qk_norm[l], d_model)
        proj = _proj_call(h, attn_norm[l], w_in[l].astype(BF16), col_gain, group_mean)
        ya, yd, yc = _attention_calls(proj, batch, t5_bias, diff_qk_norm[l], diff_lambda[l],
                                      diff_subln[l], ca_qk_norm[l], ca_rel_bias[l], lam_init)
        r_w = jnp.concatenate([router_group_w[l], router_expert_w[l]], axis=1).astype(F32)
        r_w = jnp.pad(r_w, ((0, 0), (0, LANES - r_w.shape[1])))
        r_hi, r_lo = _split_bf16(r_w)
        r_b = jnp.concatenate([router_group_b[l], router_expert_b[l]]).astype(F32)
        r_b = jnp.pad(r_b, (0, LANES - r_b.shape[0])).reshape(1, LANES)
        h2, f, meta, meta_t, count_row = _out_proj_call(h, ya, yd, yc, w_out[l].astype(BF16),
                                                        ffn_norm[l], r_hi, r_lo, r_b)
        h = _moe(h2, f, meta, meta_t, count_row, expert_w_gate, expert_w_up, expert_w_down, l)
    return h.reshape(batch, seq, d_model)
```

```python
import functools
import math

import jax
import jax.numpy as jnp
from jax import lax
from jax.experimental import pallas as pl
from jax.experimental.pallas import tpu as pltpu

F32 = jnp.float32
BF16 = jnp.bfloat16

LANES = 128
MXU_WIDTH = 256
HEAD_DIM = 64
CHUNK = 64
CA_LEFT_CHUNKS = 8
CA_REL_CLIP = 128
T5_BUCKETS = 32
T5_MAX_DIST = 256
N_GROUPS = 4
EXPERTS_PER_GROUP = 8
N_EXPERTS = N_GROUPS * EXPERTS_PER_GROUP
TOP_K_INNER = 2
EPS = 1e-6
NEG = -1e30
LOG2E = math.log2(math.e)
Q_FOLD = HEAD_DIM ** -0.5 * LOG2E

ROW_TILE = 512
ATT_TILE = 256
MOE_TILE = 512
VMEM_LIMIT = 48 * 1024 * 1024


def _cparams(n_axes):
    return pltpu.CompilerParams(dimension_semantics=("arbitrary",) * n_axes,
                                vmem_limit_bytes=VMEM_LIMIT)


def _split_bf16(x):
    hi = x.astype(BF16)
    lo = (x - hi.astype(F32)).astype(BF16)
    return hi, lo


def _col_kind(col, d_model):
    sb_w, df_w = d_model // 4, d_model // 2
    sb_end = 3 * sb_w
    df_end = sb_end + 3 * df_w
    if col < sb_w:
        return "scale"
    if col < sb_end:
        return "plain"
    if col < sb_end + 2 * df_w:
        return "norm"
    if col < df_end:
        return "plain"
    if col < df_end + 2 * sb_w:
        return "norm"
    return "plain"


def _proj_kernel(h_ref, g_ref, w_ref, cg_ref, gm_ref, o_ref, *, d_model, col_chunk):
    x = h_ref[...]
    ms = jnp.mean(x * x, axis=-1, keepdims=True)
    xn = (x * lax.rsqrt(ms + EPS) * g_ref[...]).astype(BF16)
    n_out = w_ref.shape[1]
    gm = gm_ref[...]
    width = gm.shape[0]
    for c0 in range(0, n_out, col_chunk):
        acc = jnp.dot(xn, w_ref[:, c0:c0 + col_chunk], preferred_element_type=F32)
        for s0 in range(0, col_chunk, width):
            col = c0 + s0
            blk = acc[:, s0:s0 + width]
            kind = _col_kind(col, d_model)
            assert all(_col_kind(c, d_model) == kind for c in range(col, col + width, LANES))
            if kind == "norm":
                msq = jnp.dot((blk * blk).astype(BF16), gm, preferred_element_type=F32)
                blk = blk * lax.rsqrt(msq + EPS) * cg_ref[:, col:col + width]
            elif kind == "scale":
                blk = blk * Q_FOLD
            o_ref[:, col:col + width] = blk.astype(BF16)


def _proj_call(h, gain, w_bf, col_gain, group_mean):
    n_tok, d_model = h.shape
    n_out = w_bf.shape[1]
    tm = ROW_TILE
    return pl.pallas_call(
        functools.partial(_proj_kernel, d_model=d_model, col_chunk=512),
        grid=(n_tok // tm,),
        in_specs=[pl.BlockSpec((tm, d_model), lambda i: (i, 0)),
                  pl.BlockSpec((1, d_model), lambda i: (0, 0)),
                  pl.BlockSpec((d_model, n_out), lambda i: (0, 0)),
                  pl.BlockSpec((1, n_out), lambda i: (0, 0)),
                  pl.BlockSpec(group_mean.shape, lambda i: (0, 0))],
        out_specs=pl.BlockSpec((tm, n_out), lambda i: (i, 0)),
        out_shape=jax.ShapeDtypeStruct((n_tok, n_out), BF16),
        compiler_params=_cparams(1), name="norm_in_proj")(
            h, gain.reshape(1, d_model), w_bf, col_gain, group_mean)


def _stack_halves(q):
    lane = lax.broadcasted_iota(jnp.int32, q.shape, 1)
    zero = jnp.zeros_like(q)
    return jnp.concatenate([jnp.where(lane < HEAD_DIM, q, zero),
                            jnp.where(lane >= HEAD_DIM, q, zero)], axis=0)


def _scores(q2, k):
    return lax.dot_general(q2, k, (((1,), (1,)), ((), ())), preferred_element_type=F32)


def _lane_tile(x, width):
    reps = width // LANES
    return x if reps == 1 else jnp.concatenate([x] * reps, axis=1)


def _lane_fold(x, op):
    out = x[:, :LANES]
    for c in range(LANES, x.shape[1], LANES):
        out = op(out, x[:, c:c + LANES])
    return out


def _rows_tile(ref, j, tk):
    return ref[0, pl.ds(pl.multiple_of(j * tk, tk), tk), :]


def _for_each(n, fn, ways):
    def group(g, c):
        for w in range(ways):
            fn(g * ways + w)
        return c

    lax.fori_loop(0, n // ways, group, 0)
    part = ways // 2
    while part >= 1:
        start = n // (2 * part) * (2 * part)

        def tail(start=start, part=part):
            for w in range(part):
                fn(start + w)

        pl.when(n % (2 * part) >= part)(tail)
        part //= 2


def _merge_halves(y, tq):
    lane = lax.broadcasted_iota(jnp.int32, (tq, LANES), 1)
    return jnp.where(lane < HEAD_DIM, y[:tq], y[tq:])


SOFTPLUS_CLAMP = 126.0
F32_EXP2_UNDERFLOW = 152.0


def _sb_attend(i, q_ref, k_ref, v_ref, u_ref, o_ref, acc_ref, skipped_ref, *, tq):
    q2 = _stack_halves(q_ref[0])
    rows = 2 * tq
    acc_ref[...] = jnp.zeros(acc_ref.shape, F32)
    skipped_ref[...] = jnp.zeros(skipped_ref.shape, F32)

    def tile(j, diagonal):
        z = _scores(q2, _rows_tile(k_ref, j, tq))
        sp = jnp.maximum(jnp.log(1.0 + jnp.exp2(jnp.minimum(z, SOFTPLUS_CLAMP))) * LOG2E, z)
        if diagonal:
            row = lax.broadcasted_iota(jnp.int32, (rows, tq), 0)
            col = lax.broadcasted_iota(jnp.int32, (rows, tq), 1)
            earlier = col < jnp.where(row >= tq, row - tq, row)
            sp = jnp.where(earlier, sp, 0.0)
        hi, lo = _split_bf16(sp)
        suffix = jnp.dot(jnp.concatenate([hi, lo], axis=1), u_ref[...],
                         preferred_element_type=F32)
        skipped = skipped_ref[...]
        w = jnp.exp2(z - suffix - _lane_tile(skipped, tq))
        if diagonal:
            w = jnp.where(earlier, w, 0.0)
        acc_ref[...] += jnp.dot(w.astype(BF16), _rows_tile(v_ref, j, tq),
                                preferred_element_type=F32)
        skipped = skipped + jnp.broadcast_to(suffix[:, :1], skipped.shape)
        skipped_ref[...] = skipped
        return jnp.min(skipped)

    def more(state):
        t, least_skipped = state
        return (t <= i) & (least_skipped < F32_EXP2_UNDERFLOW)

    def step(state):
        t, _ = state
        return t + 1, tile(i - t, False)

    lax.while_loop(more, step, (jnp.int32(1), tile(i, True)))
    o_ref[0] = _merge_halves(acc_ref[...], tq).astype(o_ref.dtype)


BIAS_FAR, LOGIT_MAX, LOGIT_SPREAD = 0, 1, 2
SAFE_EXP2_RANGE = 80.0


def _qk_norm_reach(qk_norm):
    g = jnp.abs(qk_norm.astype(F32))
    return 1.01 * HEAD_DIM * Q_FOLD * jnp.max(g[0]) * jnp.max(g[1])


def _logit_stats(far, bias_max, bias_min, reach):
    return jnp.stack([far, bias_max + reach, bias_max - bias_min + 2.0 * reach])


def _diff_attend(h, i, stats_ref, q_ref, k_ref, v_ref, near_ref, lam_ref, subln_ref, o_ref,
                 s_ref, p_ref, m_ref, l_ref, acc_ref, *, tq, lam_init):
    q2 = _stack_halves(q_ref[0])
    rows = 2 * tq
    far = stats_ref[BIAS_FAR, h]
    bounded = stats_ref[LOGIT_SPREAD, h] <= SAFE_EXP2_RANGE
    lmb = lam_ref[...]
    lam = (jnp.exp(jnp.sum(lmb[0:1] * lmb[1:2], axis=-1, keepdims=True))
           - jnp.exp(jnp.sum(lmb[2:3] * lmb[3:4], axis=-1, keepdims=True)) + lam_init)
    l_ref[...] = jnp.zeros(l_ref.shape, F32)
    acc_ref[...] = jnp.zeros(acc_ref.shape, F32)

    def biased_scores(j, bias):
        s = _scores(q2, _rows_tile(k_ref, j, tq))
        if bias is None:
            return s
        return (s.reshape(2, tq, tq) + bias[None]).reshape(rows, tq)

    @pl.when(bounded)
    def _():
        shift = stats_ref[LOGIT_MAX, h]

        def probs(j, bias, shift):
            p = jnp.exp2(biased_scores(j, bias) - shift)
            l_ref[...] += _lane_fold(p, jnp.add)
            p_ref[j] = p.astype(BF16)

        probs(i, near_ref[0, 0], shift)
        pl.when(i >= 1)(lambda: probs(i - 1, near_ref[0, 1], shift))
        _for_each(jnp.maximum(i - 1, 0), lambda j: probs(j, None, shift - far), 4)

        inv_l = 1.0 / jnp.sum(l_ref[...], axis=1, keepdims=True)
        m_ref[0:tq] = jnp.broadcast_to(inv_l[:tq], (tq, LANES))
        m_ref[tq:rows] = jnp.broadcast_to(-lam * inv_l[tq:], (tq, LANES))

        def weigh(j):
            p = p_ref[j]
            w = (p[:tq].astype(F32) * _lane_tile(m_ref[0:tq], tq)
                 + p[tq:].astype(F32) * _lane_tile(m_ref[tq:rows], tq))
            acc_ref[0:tq] += jnp.dot(w.astype(BF16), _rows_tile(v_ref, j, tq),
                                     preferred_element_type=F32)

        _for_each(i + 1, weigh, 4)

    @pl.when(jnp.logical_not(bounded))
    def _():
        m_ref[...] = jnp.full(m_ref.shape, NEG, F32)

        def score(j, bias):
            s = biased_scores(j, bias)
            if bias is None:
                s = s + far
            s_ref[j] = s
            m_ref[...] = jnp.maximum(m_ref[...], _lane_fold(s, jnp.maximum))

        score(i, near_ref[0, 0])
        pl.when(i >= 1)(lambda: score(i - 1, near_ref[0, 1]))
        _for_each(jnp.maximum(i - 1, 0), lambda j: score(j, None), 4)
        m_ref[...] = jnp.broadcast_to(jnp.max(m_ref[...], axis=1, keepdims=True), m_ref.shape)

        def weigh(j):
            p = jnp.exp2(s_ref[j] - _lane_tile(m_ref[...], tq))
            l_ref[...] += _lane_fold(p, jnp.add)
            acc_ref[...] += jnp.dot(p.astype(BF16), _rows_tile(v_ref, j, tq),
                                    preferred_element_type=F32)

        _for_each(i + 1, weigh, 4)
        y = acc_ref[...] / jnp.sum(l_ref[...], axis=1, keepdims=True)
        acc_ref[0:tq] = y[:tq] - lam * y[tq:]

    y = acc_ref[0:tq]
    ms = jnp.mean(y * y, axis=-1, keepdims=True)
    y = y * lax.rsqrt(ms + EPS) * subln_ref[...] * (1.0 - lam_init)
    o_ref[0] = y.astype(o_ref.dtype)


def _ca_attend(h, i, stats_ref, q_ref, k_ref, v_ref, bias_ref, o_ref, *, tq, n_near):
    q2 = _stack_halves(q_ref[0])
    bounded = stats_ref[LOGIT_SPREAD, h] <= SAFE_EXP2_RANGE

    def scores(d):
        j = jnp.maximum(i - d, 0)
        s = _scores(q2, _rows_tile(k_ref, j, tq)) + bias_ref[0, d]
        if d > 0:
            s = s + jnp.where(i >= d, 0.0, NEG)
        return j, s

    def attend(tiles, shift):
        l = None
        acc = None
        for j, s in tiles:
            p = jnp.exp2(s - shift)
            pv = jnp.dot(p.astype(BF16), _rows_tile(v_ref, j, tq), preferred_element_type=F32)
            psum = _lane_fold(p, jnp.add)
            l = psum if l is None else l + psum
            acc = pv if acc is None else acc + pv
        y = acc / jnp.sum(l, axis=1, keepdims=True)
        o_ref[0] = _merge_halves(y, tq).astype(o_ref.dtype)

    @pl.when(bounded)
    def _():
        attend((scores(d) for d in range(n_near)), stats_ref[LOGIT_MAX, h])

    @pl.when(jnp.logical_not(bounded))
    def _():
        tiles = [scores(d) for d in range(n_near)]
        m = _lane_fold(tiles[0][1], jnp.maximum)
        for _, s in tiles[1:]:
            m = jnp.maximum(m, _lane_fold(s, jnp.maximum))
        m = jnp.broadcast_to(jnp.max(m, axis=1, keepdims=True), m.shape)
        attend(tiles, _lane_tile(m, tq))


def _mixers_kernel(*refs, tq, lam_init, n_near, df_per_step):
    refs = list(refs)
    take = lambda n: [refs.pop(0) for _ in range(n)]
    diff_stats_ref, ca_stats_ref = take(2)
    sb_in = take(4)
    diff_in = [take(4) for _ in range(df_per_step)]
    lam_ref, subln_ref = take(2)
    ca_in = take(4)
    ya_ref, yd_ref, yc_ref = take(3)
    s_ref, p_ref, m_ref, l_ref, acc_ref, skipped_ref = refs
    g = pl.program_id(1)
    i = pl.program_id(2)
    _sb_attend(i, *sb_in, ya_ref, acc_ref, skipped_ref, tq=tq)
    for e, (q_ref, k_ref, v_ref, near_ref) in enumerate(diff_in):
        _diff_attend(df_per_step * g + e, i, diff_stats_ref, q_ref, k_ref, v_ref, near_ref,
                     lam_ref, subln_ref, yd_ref.at[:, :, pl.ds(e * LANES, LANES)],
                     s_ref, p_ref, m_ref, l_ref, acc_ref, tq=tq, lam_init=lam_init)
    _ca_attend(g, i, ca_stats_ref, *ca_in, yc_ref, tq=tq, n_near=n_near)


def _t5_bucket(rel):
    nb = T5_BUCKETS // 2
    max_exact = nb // 2
    ret = jnp.where(rel > 0, nb, 0)
    n = jnp.abs(rel)
    large = max_exact + (jnp.log(jnp.maximum(n, 1).astype(F32) / max_exact)
                         / math.log(T5_MAX_DIST / max_exact) * (nb - max_exact)).astype(jnp.int32)
    large = jnp.minimum(large, nb - 1)
    return ret + jnp.where(n < max_exact, n, large)


def _toeplitz_tile(bias_of_rel, tq, d):
    span = 2 * tq
    x = jnp.arange(span, dtype=jnp.int32)
    x = jnp.where(x < tq, x, x - span)
    vec = jnp.transpose(bias_of_rel(x - d * tq)).astype(F32) * LOG2E
    flat = jnp.tile(vec, (1, tq))[:, :tq * (span - 1)]
    return flat.reshape(vec.shape[0], tq, span - 1)[:, :, :tq]


def _diff_bias_tables(t5_bias, qk_norm, tq):
    assert tq >= T5_MAX_DIST and tq % CHUNK == 0
    r = jnp.arange(tq)[:, None]
    c = jnp.arange(tq)[None, :]
    bias_of_rel = lambda rel: t5_bias[_t5_bucket(rel)]
    diag = jnp.where((c // CHUNK) <= (r // CHUNK), _toeplitz_tile(bias_of_rel, tq, 0), NEG)
    scaled = t5_bias.astype(F32) * LOG2E
    far = scaled[_t5_bucket(jnp.int32(-2 * tq))]
    stats = _logit_stats(far, jnp.max(scaled, axis=0), jnp.min(scaled, axis=0),
                         _qk_norm_reach(qk_norm))
    return jnp.stack([diag, _toeplitz_tile(bias_of_rel, tq, 1)], axis=1), stats


def _ca_bias_tables(rel_bias, qk_norm, tq):
    assert (CA_LEFT_CHUNKS * CHUNK) % tq == 0 and tq % CHUNK == 0
    n_near = CA_LEFT_CHUNKS * CHUNK // tq + 1
    r = jnp.arange(tq)[:, None]
    c = jnp.arange(tq)[None, :]
    bias_of_rel = lambda rel: rel_bias[jnp.clip(rel, -CA_REL_CLIP, CA_REL_CLIP) + CA_REL_CLIP]
    tiles = []
    for d in range(n_near):
        gap = d * (tq // CHUNK) + r // CHUNK - c // CHUNK
        tiles.append(jnp.where((gap >= 0) & (gap <= CA_LEFT_CHUNKS),
                               _toeplitz_tile(bias_of_rel, tq, d), NEG))
    t = jnp.stack(tiles, axis=1)
    n_heads = t.shape[0]
    t = t.reshape(n_heads // 2, 2, n_near, tq, tq).transpose(0, 2, 1, 3, 4)
    scaled = (rel_bias.astype(F32) * LOG2E).reshape(rel_bias.shape[0], n_heads // 2, 2)
    top = jnp.max(scaled, axis=(0, 2))
    stats = _logit_stats(jnp.zeros_like(top), top, jnp.min(scaled, axis=(0, 2)),
                         _qk_norm_reach(qk_norm))
    return t.reshape(n_heads // 2, n_near, 2 * tq, tq), stats, n_near


def _attention_calls(proj, batch, t5_bias, diff_qk_norm, diff_lambda, diff_subln, ca_qk_norm,
                     ca_rel_bias, lam_init):
    n_tok, n_proj = proj.shape
    seq = n_tok // batch
    d_model = n_proj // 3
    sb_w = d_model // 4
    proj3 = proj.reshape(batch, seq, n_proj)
    tq = ATT_TILE
    nq = seq // tq
    sb_blocks = sb_w // LANES
    df_heads = (d_model // 2) // LANES
    sb_q0 = 0
    df_q0 = 3 * sb_blocks
    ca_q0 = df_q0 + 3 * df_heads

    assert df_heads % sb_blocks == 0
    df_per_step = df_heads // sb_blocks

    def qkv_specs(q0, stride, per_step=1, e=0):
        col = lambda g: q0 + per_step * g + e
        return [pl.BlockSpec((1, tq, LANES), lambda b, g, i: (b, i, col(g))),
                pl.BlockSpec((1, seq, LANES), lambda b, g, i: (b, 0, stride + col(g))),
                pl.BlockSpec((1, seq, LANES), lambda b, g, i: (b, 0, 2 * stride + col(g)))]

    smem = pl.BlockSpec(memory_space=pltpu.SMEM)
    out_spec = lambda w: pl.BlockSpec((1, tq, w), lambda b, g, i: (b, i, g))
    stacked = pltpu.VMEM((2 * tq, LANES), F32)

    lower = (jnp.arange(tq)[:, None] >= jnp.arange(tq)[None, :]).astype(BF16)
    u = jnp.concatenate([lower, lower], axis=0)
    near, bias_stats = _diff_bias_tables(t5_bias, diff_qk_norm, tq)
    ca_bias, ca_stats, n_near = _ca_bias_tables(ca_rel_bias, ca_qk_norm, tq)

    in_specs = [smem, smem] + qkv_specs(sb_q0, sb_blocks) + [
        pl.BlockSpec(u.shape, lambda b, g, i: (0, 0))]
    args = [bias_stats, ca_stats, proj3, proj3, proj3, u]
    for e in range(df_per_step):
        in_specs += qkv_specs(df_q0, df_heads, df_per_step, e) + [
            pl.BlockSpec((1, 2, tq, tq), lambda b, g, i, e=e: (df_per_step * g + e, 0, 0, 0))]
        args += [proj3, proj3, proj3, near]
    in_specs += [pl.BlockSpec((4, HEAD_DIM), lambda b, g, i: (0, 0)),
                 pl.BlockSpec((1, LANES), lambda b, g, i: (0, 0))]
    args += [diff_lambda.astype(F32), diff_subln.astype(F32).reshape(1, LANES)]
    in_specs += qkv_specs(ca_q0, sb_blocks) + [
        pl.BlockSpec((1, n_near, 2 * tq, tq), lambda b, g, i: (g, 0, 0, 0))]
    args += [proj3, proj3, proj3, ca_bias]

    ya, yd, yc = pl.pallas_call(
        functools.partial(_mixers_kernel, tq=tq, lam_init=lam_init, n_near=n_near,
                          df_per_step=df_per_step),
        grid=(batch, sb_blocks, nq),
        in_specs=in_specs,
        out_specs=[out_spec(LANES), out_spec(df_per_step * LANES), out_spec(LANES)],
        out_shape=[jax.ShapeDtypeStruct((batch, seq, sb_w), BF16),
                   jax.ShapeDtypeStruct((batch, seq, d_model // 2), BF16),
                   jax.ShapeDtypeStruct((batch, seq, sb_w), BF16)],
        scratch_shapes=[pltpu.VMEM((nq, 2 * tq, tq), F32), pltpu.VMEM((nq, 2 * tq, tq), BF16),
                        stacked, stacked, stacked, stacked],
        compiler_params=_cparams(3), name="attention_mixers")(*args)
    return (ya.reshape(n_tok, sb_w), yd.reshape(n_tok, d_model // 2), yc.reshape(n_tok, sb_w))


META_E, META_GATE, META_RANK = 0, 2, 4
META_ROWS = 8


def _row_min_lane(mask, lane_f):
    return jnp.min(jnp.where(mask, lane_f, float(LANES)), axis=1, keepdims=True)


def _out_proj_kernel(h_ref, ya_ref, yd_ref, yc_ref, wo_ref, g_ref, rhi_ref, rlo_ref, rb_ref,
                     ltri_ref, h2_ref, f_ref, meta_ref, meta_t_ref, count_ref, run_ref):
    wa = ya_ref.shape[1]
    wd = yd_ref.shape[1]
    h2 = (h_ref[...]
          + jnp.dot(ya_ref[...], wo_ref[0:wa, :], preferred_element_type=F32)
          + jnp.dot(yd_ref[...], wo_ref[wa:wa + wd, :], preferred_element_type=F32)
          + jnp.dot(yc_ref[...], wo_ref[wa + wd:, :], preferred_element_type=F32))
    h2_ref[...] = h2
    ms = jnp.mean(h2 * h2, axis=-1, keepdims=True)
    f = h2 * lax.rsqrt(ms + EPS) * g_ref[...]
    f_ref[...] = f
    f_hi, f_lo = _split_bf16(f)
    both = jnp.dot(f_hi, rlo_ref[...], preferred_element_type=F32)
    logits = (both[:, :LANES] + both[:, LANES:]
              + jnp.dot(f_lo, rhi_ref[...], preferred_element_type=F32) + rb_ref[...])

    lane = lax.broadcasted_iota(jnp.int32, logits.shape, 1)
    lane_f = lane.astype(F32)
    is_group = lane < N_GROUPS
    gl = jnp.where(is_group, logits, NEG)
    g_max = jnp.max(gl, axis=1, keepdims=True)
    g_sel = _row_min_lane(gl == g_max, lane_f)
    g_w = 1.0 / jnp.sum(jnp.where(is_group, jnp.exp(logits - g_max), 0.0), axis=1, keepdims=True)
    first_lane = float(N_GROUPS) + float(EXPERTS_PER_GROUP) * g_sel
    in_group = (lane_f >= first_lane) & (lane_f < first_lane + float(EXPERTS_PER_GROUP))
    el = jnp.where(in_group, logits, NEG)
    m1 = jnp.max(el, axis=1, keepdims=True)
    i1 = _row_min_lane(el == m1, lane_f)
    el = jnp.where(lane_f == i1, NEG, el)
    m2 = jnp.max(el, axis=1, keepdims=True)
    i2 = _row_min_lane(el == m2, lane_f)
    r = jnp.exp(m2 - m1)
    p1 = 1.0 / (1.0 + r)
    gate1 = g_w * p1
    gate2 = g_w * (r * p1)

    @pl.when(pl.program_id(0) == 0)
    def _():
        run_ref[...] = jnp.zeros(run_ref.shape, F32)

    hit1 = lane_f == i1
    hit2 = lane_f == i2
    one_hot = jnp.where(hit1 | hit2, 1.0, 0.0)
    pos = run_ref[...] + jnp.dot(ltri_ref[...], one_hot.astype(BF16), preferred_element_type=F32)
    rank1 = jnp.sum(jnp.where(hit1, pos, 0.0), axis=1, keepdims=True)
    rank2 = jnp.sum(jnp.where(hit2, pos, 0.0), axis=1, keepdims=True)
    run_ref[...] += jnp.sum(one_hot, axis=0, keepdims=True)
    count_ref[...] = run_ref[...]

    meta = jnp.zeros(logits.shape, F32)
    for at, val in ((META_E, i1 - N_GROUPS), (META_E + 1, i2 - N_GROUPS), (META_GATE, gate1),
                    (META_GATE + 1, gate2), (META_RANK, rank1), (META_RANK + 1, rank2)):
        meta = jnp.where(lane == at, val, meta)
    meta_ref[...] = meta
    meta_t_ref[...] = jnp.transpose(meta)[:META_ROWS]


def _out_proj_call(h, ya, yd, yc, wo_bf, gain, r_hi, r_lo, r_bias):
    n_tok, d_model = h.shape
    tm = ROW_TILE
    row = lambda w: pl.BlockSpec((tm, w), lambda i: (i, 0))
    full = lambda a: pl.BlockSpec(a.shape, lambda i: (0, 0))
    gain = gain.reshape(1, d_model)
    ltri = (jnp.arange(tm)[:, None] > jnp.arange(tm)[None, :]).astype(BF16)
    return pl.pallas_call(
        _out_proj_kernel,
        grid=(n_tok // tm,),
        in_specs=[row(d_model), row(ya.shape[1]), row(yd.shape[1]), row(yc.shape[1]),
                  full(wo_bf), full(gain), full(r_hi), full(r_lo), full(r_bias), full(ltri)],
        out_specs=[row(d_model), row(d_model), row(LANES),
                   pl.BlockSpec((META_ROWS, tm), lambda i: (0, i)),
                   pl.BlockSpec((1, LANES), lambda i: (0, 0))],
        out_shape=[jax.ShapeDtypeStruct((n_tok, d_model), F32),
                   jax.ShapeDtypeStruct((n_tok, d_model), F32),
                   jax.ShapeDtypeStruct((n_tok, LANES), F32),
                   jax.ShapeDtypeStruct((META_ROWS, n_tok), F32),
                   jax.ShapeDtypeStruct((1, LANES), F32)],
        scratch_shapes=[pltpu.VMEM((1, LANES), F32)],
        compiler_params=_cparams(1), name="out_proj_router")(
            h, ya, yd, yc, wo_bf, gain, r_hi, r_lo, r_bias, ltri)


def _dispatch_plan(meta_t, count_row, tb, tm):
    n_tok = meta_t.shape[1]
    counts = count_row[0, N_GROUPS:N_GROUPS + N_EXPERTS].astype(jnp.int32)
    padded = (counts + tb - 1) // tb * tb
    e = jnp.arange(N_EXPERTS)
    pend = jnp.sum(jnp.where(e[:, None] >= e[None, :], padded[None, :], 0), axis=1)
    pstart = pend - padded
    n_rows = n_tok * TOP_K_INNER + N_EXPERTS * tb
    block_start = jnp.arange(n_rows // tb, dtype=jnp.int32) * tb
    blk_expert = jnp.minimum(jnp.sum((pend[None, :] <= block_start[:, None]).astype(jnp.int32), axis=1),
                             N_EXPERTS - 1)
    n_used = pend[-1:] // tb
    expert = meta_t[META_E:META_E + TOP_K_INNER].astype(jnp.int32)
    rank = meta_t[META_RANK:META_RANK + TOP_K_INNER].astype(jnp.int32)
    seg_start = jnp.sum(jnp.where(expert[..., None] == e, pstart, 0), axis=-1)
    idx = jnp.transpose(seg_start + rank).reshape(n_tok // tm, 1, tm * TOP_K_INNER)
    return pend.astype(jnp.int32), blk_expert, n_used, idx, n_rows


def _for_each_assignment(tm, fn):
    def body(r, c):
        for k in range(TOP_K_INNER):
            fn(r, k)
        return c

    lax.fori_loop(0, tm, body, 0, unroll=8)


def _dispatch_kernel(pend_ref, idx_ref, f_ref, xs_ref, zero_ref, row_sem, zero_sem, *, tb):
    tm = f_ref.shape[0]

    @pl.when(pl.program_id(0) == 0)
    def _():
        zero_ref[...] = jnp.zeros(zero_ref.shape, F32)

        def block_copy(start):
            return pltpu.make_async_copy(zero_ref, xs_ref.at[pl.ds(pl.multiple_of(start, tb), tb)],
                                         zero_sem)

        for e in range(N_EXPERTS):
            block_copy(jnp.maximum(pend_ref[e] - tb, 0)).start()
        for e in range(N_EXPERTS):
            block_copy(jnp.maximum(pend_ref[e] - tb, 0)).wait()

        def fill(b, c):
            block_copy(b * tb).start()
            block_copy(b * tb).wait()
            return c

        first_unused = lax.shift_right_logical(pend_ref[N_EXPERTS - 1], tb.bit_length() - 1)
        lax.fori_loop(first_unused, xs_ref.shape[0] // tb, fill, 0)

    def row_copy(r, k):
        return pltpu.make_async_copy(f_ref.at[pl.ds(r, 1)], xs_ref.at[pl.ds(idx_ref[0, 0, r * TOP_K_INNER + k], 1)],
                                     row_sem)

    _for_each_assignment(tm, lambda r, k: row_copy(r, k).start())
    _for_each_assignment(tm, lambda r, k: row_copy(r, k).wait())


def _dispatch_call(f, pend, idx, n_rows):
    n_tok, d_model = f.shape
    tm = idx.shape[2] // TOP_K_INNER
    tb = MOE_TILE
    assert tb & (tb - 1) == 0
    grid_spec = pltpu.PrefetchScalarGridSpec(
        num_scalar_prefetch=1,
        grid=(n_tok // tm,),
        in_specs=[pl.BlockSpec((1,) + idx.shape[1:], lambda i, pe: (i, 0, 0),
                               memory_space=pltpu.SMEM),
                  pl.BlockSpec((tm, d_model), lambda i, pe: (i, 0))],
        out_specs=pl.BlockSpec(memory_space=pl.ANY),
        scratch_shapes=[pltpu.VMEM((tb, d_model), F32), pltpu.SemaphoreType.DMA(()),
                        pltpu.SemaphoreType.DMA(())])
    return pl.pallas_call(
        functools.partial(_dispatch_kernel, tb=tb), grid_spec=grid_spec,
        out_shape=jax.ShapeDtypeStruct((n_rows, d_model), F32),
        compiler_params=_cparams(1), name="moe_dispatch")(pend, idx, f)


def _expert_kernel(blk_expert_ref, n_used_ref, x_ref, wg_ref, wu_ref, wd_ref, y_ref,
                   wg_bf, wu_bf, wd_bf):
    i = pl.program_id(0)
    used = i < n_used_ref[0]
    e = blk_expert_ref[i]
    e_prev = blk_expert_ref[jnp.maximum(i - 1, 0)]

    @pl.when(used & ((i == 0) | (e != e_prev)))
    def _():
        wg_bf[...] = wg_ref[0, 0].astype(BF16)
        wu_bf[...] = wu_ref[0, 0].astype(BF16)
        wd_bf[...] = wd_ref[0, 0].astype(BF16)

    @pl.when(used)
    def _():
        x = x_ref[...].astype(BF16)
        g = jnp.dot(x, wg_bf[...], preferred_element_type=F32)
        u = jnp.dot(x, wu_bf[...], preferred_element_type=F32)
        hid = (g / (1.0 + jnp.exp(-g)) * u).astype(BF16)
        y_ref[...] = jnp.dot(hid, wd_bf[...], preferred_element_type=F32)

    @pl.when(jnp.logical_not(used))
    def _():
        y_ref[...] = jnp.zeros(y_ref.shape, F32)


def _expert_call(xs, blk_expert, n_used, w_gate, w_up, w_down, layer):
    n_rows, d_model = xs.shape
    d_exp = w_gate.shape[3]
    tb = MOE_TILE
    blk = lambda i, be, nu: jnp.minimum(i, nu[0] - 1)
    weight = lambda shape: pl.BlockSpec(
        (1, 1) + shape, lambda i, be, nu: (layer, be[blk(i, be, nu)], 0, 0))
    grid_spec = pltpu.PrefetchScalarGridSpec(
        num_scalar_prefetch=2,
        grid=(n_rows // tb,),
        in_specs=[pl.BlockSpec((tb, d_model), lambda i, be, nu: (blk(i, be, nu), 0)),
                  weight((d_model, d_exp)), weight((d_model, d_exp)), weight((d_exp, d_model))],
        out_specs=pl.BlockSpec((tb, d_model), lambda i, be, nu: (i, 0)),
        scratch_shapes=[pltpu.VMEM((d_model, d_exp), BF16), pltpu.VMEM((d_model, d_exp), BF16),
                        pltpu.VMEM((d_exp, d_model), BF16)])
    return pl.pallas_call(
        _expert_kernel, grid_spec=grid_spec,
        out_shape=jax.ShapeDtypeStruct(xs.shape, F32),
        compiler_params=_cparams(1), name="expert_mlp")(
            blk_expert, n_used, xs, w_gate, w_up, w_down)


def _combine_kernel(idx_ref, idx_next_ref, h_ref, meta_ref, y_ref, o_ref, ybuf_ref, sems):
    i = pl.program_id(0)
    tm = h_ref.shape[0]
    slot = i % 2

    def row_copy(index_ref, slot, r, k):
        src = index_ref[0, 0, r * TOP_K_INNER + k]
        return pltpu.make_async_copy(y_ref.at[pl.ds(src, 1)],
                                     ybuf_ref.at[slot, k, pl.ds(r, 1)], sems.at[slot])

    def request(index_ref, slot):
        _for_each_assignment(tm, lambda r, k: row_copy(index_ref, slot, r, k).start())

    pl.when(i == 0)(lambda: request(idx_ref, 0))
    pl.when(i + 1 < pl.num_programs(0))(lambda: request(idx_next_ref, 1 - slot))
    _for_each_assignment(tm, lambda r, k: row_copy(idx_ref, slot, r, k).wait())
    meta = meta_ref[...]
    out = h_ref[...]
    for k in range(TOP_K_INNER):
        out = out + meta[:, META_GATE + k:META_GATE + k + 1] * ybuf_ref[slot, k]
    o_ref[...] = out


def _combine_call(h2, meta, y_rows, idx):
    n_tok, d_model = h2.shape
    tm = idx.shape[2] // TOP_K_INNER
    n_tiles = n_tok // tm
    idx_block = (1,) + idx.shape[1:]
    return pl.pallas_call(
        _combine_kernel,
        grid=(n_tiles,),
        in_specs=[pl.BlockSpec(idx_block, lambda i: (i, 0, 0), memory_space=pltpu.SMEM),
                  pl.BlockSpec(idx_block, lambda i: (jnp.minimum(i + 1, n_tiles - 1), 0, 0),
                               memory_space=pltpu.SMEM),
                  pl.BlockSpec((tm, d_model), lambda i: (i, 0)),
                  pl.BlockSpec((tm, LANES), lambda i: (i, 0)),
                  pl.BlockSpec(memory_space=pl.ANY)],
        out_specs=pl.BlockSpec((tm, d_model), lambda i: (i, 0)),
        out_shape=jax.ShapeDtypeStruct((n_tok, d_model), F32),
        scratch_shapes=[pltpu.VMEM((2, TOP_K_INNER, tm, d_model), F32),
                        pltpu.SemaphoreType.DMA((2,))],
        compiler_params=_cparams(1), name="moe_combine")(idx, idx, h2, meta, y_rows)


def _moe(h2, f, meta, meta_t, count_row, w_gate, w_up, w_down, layer):
    pend, blk_expert, n_used, idx, n_rows = _dispatch_plan(meta_t, count_row, MOE_TILE, ROW_TILE)
    xs = _dispatch_call(f, pend, idx, n_rows)
    y_rows = _expert_call(xs, blk_expert, n_used, w_gate, w_up, w_down, layer)
    return _combine_call(h2, meta, y_rows, idx)


def _proj_col_gain(diff_qk_norm, ca_qk_norm, d_model):
    sb_w, df_w = d_model // 4, d_model // 2
    ones = lambda n: jnp.ones((n,), F32)
    rep = lambda g, n: jnp.tile(g.astype(F32), n // HEAD_DIM)
    return jnp.concatenate([
        ones(3 * sb_w),
        rep(diff_qk_norm[0], df_w) * Q_FOLD, rep(diff_qk_norm[1], df_w), ones(df_w),
        rep(ca_qk_norm[0], sb_w) * Q_FOLD, rep(ca_qk_norm[1], sb_w), ones(sb_w)]).reshape(1, -1)


def kernel(x, t5_bias, attn_norm, w_in, diff_qk_norm, diff_lambda, diff_subln, ca_qk_norm,
           ca_rel_bias, w_out, ffn_norm, router_group_w, router_group_b, router_expert_w,
           router_expert_b, expert_w_gate, expert_w_up, expert_w_down):
    batch, seq, d_model = x.shape
    depth = w_in.shape[0]
    n_tok = batch * seq
    lane_group = jnp.arange(MXU_WIDTH) // HEAD_DIM
    group_mean = ((lane_group[:, None] == lane_group[None, :]).astype(F32) / HEAD_DIM).astype(BF16)

    h = x.reshape(n_tok, d_model)
    for l in range(depth):
        lam_init = 0.8 - 0.6 * math.exp(-0.3 * l)
        col_gain = _proj_col_gain(diff_qk_norm[l], ca_qk_norm[l], d_model)
        proj = _proj_call(h, attn_norm[l], w_in[l].astype(BF16), col_gain, group_mean)
        ya, yd, yc = _attention_calls(proj, batch, t5_bias, diff_qk_norm[l], diff_lambda[l],
                                      diff_subln[l], ca_qk_norm[l], ca_rel_bias[l], lam_init)
        r_w = jnp.concatenate([router_group_w[l], router_expert_w[l]], axis=1).astype(F32)
        r_w = jnp.pad(r_w, ((0, 0), (0, LANES - r_w.shape[1])))
        r_hi, r_lo = _split_bf16(r_w)
        r_lo = jnp.concatenate([r_hi, r_lo], axis=1)
        r_b = jnp.concatenate([router_group_b[l], router_expert_b[l]]).astype(F32)
        r_b = jnp.pad(r_b, (0, LANES - r_b.shape[0])).reshape(1, LANES)
        h2, f, meta, meta_t, count_row = _out_proj_call(h, ya, yd, yc, w_out[l].astype(BF16),
                                                        ffn_norm[l], r_hi, r_lo, r_b)
        h = _moe(h2, f, meta, meta_t, count_row, expert_w_gate, expert_w_up, expert_w_down, l)
    return h.reshape(batch, seq, d_model)
```
